```python
import math
import jax, jax.numpy as jnp
from jax import lax
import numpy as np

D_MODEL = 1024
BATCH = 4
SEQ = 8192
DEPTH = 4
DEC_BATCH = 16
DEC_SEQ = 16
PAST_LEN = 1024

CHUNK = 64
D_S5 = 512
S5_GROUP = 16
S5_GROUPS = D_S5 // S5_GROUP
S5_STATE = 64
RET_HEADS = 4
RET_DK = 128
RET_DV = 128
D_RET = RET_HEADS * RET_DV
D_MIX = D_S5 + D_RET
D_IN = D_S5 + 2 * RET_HEADS * RET_DK + 2 * D_RET
N_MEM = 256
X_HEADS = 4
X_DH = D_MODEL // X_HEADS
D_FF = 2816
ROPE_BASE = 10000.0
EPS = 1e-6

kernel_name = "hybrid_s5_retention_streaming_step"

F32 = jnp.float32


def rms_norm(x, g):
    xf = x.astype(F32)
    y = xf * lax.rsqrt(jnp.mean(xf * xf, axis=-1, keepdims=True) + EPS)
    return y * g.astype(F32)


def rope(x, pos):
    half = x.shape[-1] // 2
    inv = ROPE_BASE ** (-jnp.arange(half, dtype=F32) / half)
    ang = pos.astype(F32)[:, None] * inv[None, :]
    cos = jnp.cos(ang)[None, :, None, :]
    sin = jnp.sin(ang)[None, :, None, :]
    x1, x2 = x[..., :half], x[..., half:]
    return jnp.concatenate([x1 * cos - x2 * sin, x1 * sin + x2 * cos], axis=-1)


def retention_log_decay():
    return jnp.log(1.0 - 2.0 ** (-5.0 - jnp.arange(RET_HEADS, dtype=F32)))


def retention(q, k, v, s0):
    B, L, H, dk = q.shape
    dv = v.shape[-1]
    C = CHUNK if L >= CHUNK else L
    nc = L // C
    lg = retention_log_decay()
    pos = jnp.arange(C, dtype=F32)
    q = q.reshape(B, nc, C, H, dk) * (dk ** -0.5)
    k = k.reshape(B, nc, C, H, dk)
    v = v.reshape(B, nc, C, H, dv)
    dmat = jnp.exp(lg[:, None, None] * jnp.abs(pos[:, None] - pos[None, :]))
    scores = jnp.einsum('bnihd,bnjhd->bnhij', q, k) * dmat
    inner = jnp.einsum('bnhij,bnjhe->bnihe', scores, v)
    k_dec = k * jnp.exp(lg[None, :] * (C - 1.0 - pos)[:, None])[:, :, None]
    kv = jnp.einsum('bnjhd,bnjhe->nbhde', k_dec, v)
    g_chunk = jnp.exp(lg * C)[None, :, None, None]
    if s0 is None:
        s0 = jnp.zeros((B, H, dk, dv), F32)

    def step(S, kv_c):
        return g_chunk * S + kv_c, S

    s_final, s_before = lax.scan(step, s0.astype(F32), kv)
    q_dec = q * jnp.exp(lg[None, :] * (pos + 1.0)[:, None])[:, :, None]
    cross = jnp.einsum('bnihd,nbhde->bnihe', q_dec, s_before)
    return (inner + cross).reshape(B, L, H, dv), s_final


def _complex_linear_combine(e1, e2):
    a1r, a1i, b1r, b1i = e1
    a2r, a2i, b2r, b2i = e2
    return (a2r * a1r - a2i * a1i,
            a2r * a1i + a2i * a1r,
            a2r * b1r - a2i * b1i + b2r,
            a2r * b1i + a2i * b1r + b2i)


def s5_ssm(u, lam_re, lam_im, log_step, b_re, b_im, c_re, c_im, d_skip, x0_re, x0_im):
    B, L, _ = u.shape
    uf = u.astype(F32).reshape(B, L, S5_GROUPS, S5_GROUP)
    lam_re = lam_re.astype(F32)
    lam_im = lam_im.astype(F32)
    dt = jnp.exp(log_step.astype(F32))[:, None]
    ar = lam_re * dt
    ai = lam_im * dt
    mag = jnp.exp(ar)
    abar_re = mag * jnp.cos(ai)
    abar_im = mag * jnp.sin(ai)
    den = lam_re * lam_re + lam_im * lam_im
    nr = abar_re - 1.0
    ni = abar_im
    f_re = (nr * lam_re + ni * lam_im) / den
    f_im = (ni * lam_re - nr * lam_im) / den
    b_re = b_re.astype(F32)
    b_im = b_im.astype(F32)
    bb_re = f_re[..., None] * b_re - f_im[..., None] * b_im
    bb_im = f_re[..., None] * b_im + f_im[..., None] * b_re
    bu_re = jnp.einsum('blgh,gph->blgp', uf, bb_re)
    bu_im = jnp.einsum('blgh,gph->blgp', uf, bb_im)
    a_re = jnp.broadcast_to(abar_re[None, None], (1, L, S5_GROUPS, S5_STATE))
    a_im = jnp.broadcast_to(abar_im[None, None], (1, L, S5_GROUPS, S5_STATE))
    _, _, xr, xi = lax.associative_scan(_complex_linear_combine, (a_re, a_im, bu_re, bu_im), axis=1)
    if x0_re is not None:
        t = jnp.arange(1, L + 1, dtype=F32)[:, None, None]
        pm = jnp.exp(t * ar)
        pr = pm * jnp.cos(t * ai)
        pi = pm * jnp.sin(t * ai)
        x0r = x0_re.astype(F32)[:, None]
        x0i = x0_im.astype(F32)[:, None]
        xr = xr + pr * x0r - pi * x0i
        xi = xi + pr * x0i + pi * x0r
    y = (jnp.einsum('blgp,ghp->blgh', xr, c_re.astype(F32))
         - jnp.einsum('blgp,ghp->blgh', xi, c_im.astype(F32))
         + d_skip.astype(F32) * uf)
    return y.reshape(B, L, D_S5), xr[:, -1], xi[:, -1]


def mixer_sublayer(x, pos0, s5_x0_re, s5_x0_im, ret_s0, p, i):
    B, L, _ = x.shape
    h = rms_norm(x, p['mix_norm_pre'][i]).astype(x.dtype)
    proj = (h @ p['w_in'][i]).astype(F32)
    o1 = D_S5
    o2 = o1 + RET_HEADS * RET_DK
    o3 = o2 + RET_HEADS * RET_DK
    o4 = o3 + D_RET
    u = proj[..., :o1]
    q = proj[..., o1:o2].reshape(B, L, RET_HEADS, RET_DK)
    k = proj[..., o2:o3].reshape(B, L, RET_HEADS, RET_DK)
    v = proj[..., o3:o4].reshape(B, L, RET_HEADS, RET_DV)
    g = proj[..., o4:]
    y, s5r, s5i = s5_ssm(u, p['s5_lambda_re'][i], p['s5_lambda_im'][i], p['s5_log_step'][i],
                         p['s5_b_re'][i], p['s5_b_im'][i], p['s5_c_re'][i], p['s5_c_im'][i],
                         p['s5_d'][i], s5_x0_re, s5_x0_im)
    z = jax.nn.gelu(y)
    z = z * jax.nn.sigmoid(z @ p['s5_w_glu'][i].astype(F32) + p['s5_b_glu'][i].astype(F32))
    s5_out = rms_norm(z, p['s5_out_norm'][i])
    pos = pos0 + jnp.arange(L)
    o, s_ret = retention(rope(q, pos), rope(k, pos), v, ret_s0)
    o = rms_norm(o, p['ret_out_norm'][i]).reshape(B, L, D_RET)
    ret_out = jax.nn.silu(g) * o
    mix = jnp.concatenate([s5_out, ret_out], axis=-1).astype(x.dtype) @ p['w_out'][i]
    x = x + rms_norm(mix, p['mix_norm_post'][i]).astype(x.dtype)
    return x, s5r, s5i, s_ret


def memory_kv(mem, g_mem, w_ck, w_cv):
    B = mem.shape[0]
    m = rms_norm(mem, g_mem).astype(mem.dtype)
    k = (m @ w_ck).reshape(B, N_MEM, X_HEADS, X_DH)
    v = (m @ w_cv).reshape(B, N_MEM, X_HEADS, X_DH)
    return k, v


def cross_sublayer(x, mem_k, mem_v, p, i):
    B, L, _ = x.shape
    h = rms_norm(x, p['xattn_norm_pre'][i]).astype(x.dtype)
    q = (h @ p['w_cq'][i]).reshape(B, L, X_HEADS, X_DH).astype(F32)
    s = jnp.einsum('blhd,bmhd->bhlm', q, mem_k.astype(F32)) * (X_DH ** -0.5)
    pr = jax.nn.softmax(s, axis=-1)
    o = jnp.einsum('bhlm,bmhd->blhd', pr, mem_v.astype(F32)).reshape(B, L, D_MODEL)
    out = o.astype(x.dtype) @ p['w_co'][i]
    return x + rms_norm(out, p['xattn_norm_post'][i]).astype(x.dtype)


def ffn_sublayer(x, p, i):
    h = rms_norm(x, p['ffn_norm_pre'][i]).astype(x.dtype)
    f = (jax.nn.silu(h @ p['w_gate'][i]) * (h @ p['w_up'][i])) @ p['w_down'][i]
    return x + rms_norm(f, p['ffn_norm_post'][i]).astype(x.dtype)


def trunk(x, pos0, mem_k, mem_v, s5_re0, s5_im0, ret0, p):
    s5r_all, s5i_all, ret_all = [], [], []
    for i in range(DEPTH):
        if ret0 is None:
            x, s5r, s5i, s_ret = mixer_sublayer(x, pos0, None, None, None, p, i)
        else:
            x, s5r, s5i, s_ret = mixer_sublayer(x, pos0, s5_re0[i], s5_im0[i], ret0[i], p, i)
        x = cross_sublayer(x, mem_k[i], mem_v[i], p, i)
        x = ffn_sublayer(x, p, i)
        s5r_all.append(s5r)
        s5i_all.append(s5i)
        ret_all.append(s_ret)
    return x, jnp.stack(s5r_all), jnp.stack(s5i_all), jnp.stack(ret_all)


def setup_inputs(seed: int = 0) -> dict:
    key = jax.random.key(seed)
    ks = iter(jax.random.split(key, 48))

    def nrm(shape, scale):
        return jax.random.normal(next(ks), shape, F32) * scale

    def gain(shape):
        return 1.0 + nrm(shape, 0.01)

    G, P = S5_GROUPS, S5_STATE
    return {
        'x_prompt': nrm((BATCH, SEQ, D_MODEL), 1.0),
        'x_sample': nrm((DEC_BATCH, DEC_SEQ, D_MODEL), 1.0),
        'mem_prompt': nrm((BATCH, N_MEM, D_MODEL), 1.0),
        'state_s5_re': nrm((DEPTH, DEC_BATCH, G, P), 0.1),
        'state_s5_im': nrm((DEPTH, DEC_BATCH, G, P), 0.1),
        'state_ret': nrm((DEPTH, DEC_BATCH, RET_HEADS, RET_DK, RET_DV), 1.0),
        'cache_mem_k': nrm((DEPTH, DEC_BATCH, N_MEM, X_HEADS, X_DH), 1.0),
        'cache_mem_v': nrm((DEPTH, DEC_BATCH, N_MEM, X_HEADS, X_DH), 1.0),
        'mix_norm_pre': gain((DEPTH, D_MODEL)),
        'mix_norm_post': gain((DEPTH, D_MODEL)),
        'w_in': nrm((DEPTH, D_MODEL, D_IN), D_MODEL ** -0.5),
        's5_lambda_re': -0.5 + nrm((DEPTH, G, P), 0.01),
        's5_lambda_im': jnp.pi * jnp.arange(P, dtype=F32) + nrm((DEPTH, G, P), 0.01),
        's5_log_step': jax.random.uniform(next(ks), (DEPTH, G), F32, math.log(1e-3), math.log(1e-1)),
        's5_b_re': nrm((DEPTH, G, P, S5_GROUP), (2 * S5_GROUP) ** -0.5),
        's5_b_im': nrm((DEPTH, G, P, S5_GROUP), (2 * S5_GROUP) ** -0.5),
        's5_c_re': nrm((DEPTH, G, S5_GROUP, P), (2 * P) ** -0.5),
        's5_c_im': nrm((DEPTH, G, S5_GROUP, P), (2 * P) ** -0.5),
        's5_d': nrm((DEPTH, G, S5_GROUP), 1.0),
        's5_w_glu': nrm((DEPTH, D_S5, D_S5), D_S5 ** -0.5),
        's5_b_glu': nrm((DEPTH, D_S5), 0.01),
        's5_out_norm': gain((DEPTH, D_S5)),
        'ret_out_norm': gain((DEPTH, RET_HEADS, RET_DV)),
        'w_out': nrm((DEPTH, D_MIX, D_MODEL), D_MIX ** -0.5),
        'xattn_norm_pre': gain((DEPTH, D_MODEL)),
        'xattn_norm_post': gain((DEPTH, D_MODEL)),
        'mem_norm': gain((DEPTH, D_MODEL)),
        'w_cq': nrm((DEPTH, D_MODEL, D_MODEL), D_MODEL ** -0.5),
        'w_ck': nrm((DEPTH, D_MODEL, D_MODEL), D_MODEL ** -0.5),
        'w_cv': nrm((DEPTH, D_MODEL, D_MODEL), D_MODEL ** -0.5),
        'w_co': nrm((DEPTH, D_MODEL, D_MODEL), D_MODEL ** -0.5),
        'ffn_norm_pre': gain((DEPTH, D_MODEL)),
        'ffn_norm_post': gain((DEPTH, D_MODEL)),
        'w_gate': nrm((DEPTH, D_MODEL, D_FF), D_MODEL ** -0.5),
        'w_up': nrm((DEPTH, D_MODEL, D_FF), D_MODEL ** -0.5),
        'w_down': nrm((DEPTH, D_FF, D_MODEL), D_FF ** -0.5),
    }


def reference(x_prompt, x_sample, mem_prompt, state_s5_re, state_s5_im, state_ret, cache_mem_k, cache_mem_v,
              mix_norm_pre, mix_norm_post, w_in, s5_lambda_re, s5_lambda_im, s5_log_step,
              s5_b_re, s5_b_im, s5_c_re, s5_c_im, s5_d, s5_w_glu, s5_b_glu, s5_out_norm, ret_out_norm,
              w_out, xattn_norm_pre, xattn_norm_post, mem_norm, w_cq, w_ck, w_cv, w_co,
              ffn_norm_pre, ffn_norm_post, w_gate, w_up, w_down):
    p = {
        'mix_norm_pre': mix_norm_pre, 'mix_norm_post': mix_norm_post, 'w_in': w_in,
        's5_lambda_re': s5_lambda_re, 's5_lambda_im': s5_lambda_im, 's5_log_step': s5_log_step,
        's5_b_re': s5_b_re, 's5_b_im': s5_b_im, 's5_c_re': s5_c_re, 's5_c_im': s5_c_im, 's5_d': s5_d,
        's5_w_glu': s5_w_glu, 's5_b_glu': s5_b_glu, 's5_out_norm': s5_out_norm,
        'ret_out_norm': ret_out_norm, 'w_out': w_out,
        'xattn_norm_pre': xattn_norm_pre, 'xattn_norm_post': xattn_norm_post,
        'w_cq': w_cq, 'w_co': w_co,
        'ffn_norm_pre': ffn_norm_pre, 'ffn_norm_post': ffn_norm_post,
        'w_gate': w_gate, 'w_up': w_up, 'w_down': w_down,
    }
    mk_list, mv_list = [], []
    for i in range(DEPTH):
        mk, mv = memory_kv(mem_prompt, mem_norm[i], w_ck[i], w_cv[i])
        mk_list.append(mk)
        mv_list.append(mv)
    new_mem_k_prompt = jnp.stack(mk_list)
    new_mem_v_prompt = jnp.stack(mv_list)
    y_prompt, new_s5_re_prompt, new_s5_im_prompt, new_ret_prompt = trunk(
        x_prompt, 0, new_mem_k_prompt, new_mem_v_prompt, None, None, None, p)
    y_sample, new_s5_re_sample, new_s5_im_sample, new_ret_sample = trunk(
        x_sample, PAST_LEN, cache_mem_k, cache_mem_v, state_s5_re, state_s5_im, state_ret, p)
    return (y_prompt, y_sample, new_s5_re_prompt, new_s5_im_prompt, new_ret_prompt,
            new_mem_k_prompt, new_mem_v_prompt, new_s5_re_sample, new_s5_im_sample, new_ret_sample)
```

```python
import functools
import math

import jax
import jax.numpy as jnp
from jax import lax
from jax.experimental import pallas as pl
from jax.experimental.pallas import tpu as pltpu

F32 = jnp.float32
BF16 = jnp.bfloat16

PAST_LEN = 1024
CHUNK = 64
D_S5 = 512
S5_GROUP = 16
S5_GROUPS = D_S5 // S5_GROUP
S5_STATE = 64
S5_LANES = S5_GROUPS * S5_STATE
RET_HEADS = 4
RET_DK = 128
RET_DV = 128
D_RET = RET_HEADS * RET_DV
X_HEADS = 4
ROPE_BASE = 10000.0
EPS = 1e-6

LANE = 128
MXU = 256
N_LANE_TILES = S5_LANES // LANE
SCAN_TILES = 4
VMEM_LIMIT = 56 * 1024 * 1024


def _const_spec(shape):
    zeros = (0,) * len(shape)
    return pl.BlockSpec(shape, lambda *_: zeros, pipeline_mode=pl.Buffered(1))


def _rms(x, g):
    ms = jnp.mean(x * x, axis=-1, keepdims=True)
    return x * lax.rsqrt(ms + EPS) * g


def _dot(a, b):
    return jnp.dot(a, b, preferred_element_type=F32)


def _dot_nt(a, b):
    return lax.dot_general(a, b, (((1,), (1,)), ((), ())), preferred_element_type=F32)


def _dot_tn(a, b):
    return lax.dot_general(a, b, (((0,), (0,)), ((), ())), preferred_element_type=F32)


def _gelu_tanh(x):
    c = math.sqrt(2.0 / math.pi)
    return 0.5 * x * (1.0 + jnp.tanh(c * (x + 0.044715 * (x * x * x))))


def _sigmoid(x):
    return 1.0 / (1.0 + jnp.exp(-x))


def _mixer_kernel(x_ref, cos_ref, sin_ref, s5re0_ref, s5im0_ref, ret0_ref,
                  gpre_ref, gpost_ref, win_ref, are_ref, aim_ref, wbre_ref, wbim_ref,
                  wcre_ref, wcim_ref, dskip_ref, wglu_ref, bglu_ref, gs5_ref, gret_ref, wout_ref,
                  y_ref, s5re_ref, s5im_ref, ret_ref,
                  bur_ref, bui_ref, reto_ref, *, chunk):
    nb, tb, d = x_ref.shape
    m = nb * tb

    @pl.when(pl.program_id(0) == 0)
    def _():
        s5re_ref[...] = s5re0_ref[...]
        s5im_ref[...] = s5im0_ref[...]
        ret_ref[...] = ret0_ref[...]

    x = x_ref[...].reshape(m, d)
    h = _rms(x, gpre_ref[...]).astype(BF16)

    u = _dot(h, win_ref[:, 0:D_S5])
    ub = u.astype(BF16)
    groups_per_tile = MXU // S5_STATE
    for n in range(S5_LANES // MXU):
        ks = (n * groups_per_tile * S5_GROUP) // LANE
        lhs = ub[:, ks * LANE:(ks + 1) * LANE]
        r = _dot(lhs, wbre_ref[n])
        bur_ref[2 * n] = r[:, :LANE]
        bur_ref[2 * n + 1] = r[:, LANE:]
        r = _dot(lhs, wbim_ref[n])
        bui_ref[2 * n] = r[:, :LANE]
        bui_ref[2 * n + 1] = r[:, LANE:]

    for c0 in range(0, N_LANE_TILES, SCAN_TILES):
        tiles = range(c0, c0 + SCAN_TILES)
        ar = [jnp.broadcast_to(are_ref[:, i * LANE:(i + 1) * LANE], (nb, LANE)) for i in tiles]
        ai = [jnp.broadcast_to(aim_ref[:, i * LANE:(i + 1) * LANE], (nb, LANE)) for i in tiles]
        init = tuple(s5re_ref[:, i * LANE:(i + 1) * LANE] for i in tiles) + tuple(
            s5im_ref[:, i * LANE:(i + 1) * LANE] for i in tiles)

        def step(t, carry):
            out_r, out_i = [], []
            for k, i in enumerate(tiles):
                xr, xi = carry[k], carry[SCAN_TILES + k]
                rows = pl.ds(t, nb, stride=tb)
                nr = ar[k] * xr - ai[k] * xi + bur_ref.at[i][rows, :]
                ni = ar[k] * xi + ai[k] * xr + bui_ref.at[i][rows, :]
                bur_ref.at[i][rows, :] = nr
                bui_ref.at[i][rows, :] = ni
                out_r.append(nr)
                out_i.append(ni)
            return tuple(out_r) + tuple(out_i)

        fin = lax.fori_loop(0, tb, step, init, unroll=min(tb, 8))
        for k, i in enumerate(tiles):
            s5re_ref[:, i * LANE:(i + 1) * LANE] = fin[k]
            s5im_ref[:, i * LANE:(i + 1) * LANE] = fin[SCAN_TILES + k]

    tiles_per_out = (MXU // S5_GROUP) * S5_STATE // LANE
    ys = []
    for n in range(D_S5 // MXU):
        xr = jnp.concatenate([bur_ref[n * tiles_per_out + i] for i in range(tiles_per_out)], axis=1)
        xi = jnp.concatenate([bui_ref[n * tiles_per_out + i] for i in range(tiles_per_out)], axis=1)
        ys.append(_dot(xr.astype(BF16), wcre_ref[n]) + _dot(xi.astype(BF16), wcim_ref[n]))
    y = jnp.concatenate(ys, axis=1) + dskip_ref[...] * u
    z = _gelu_tanh(y)
    z = z * _sigmoid(_dot(z.astype(BF16), wglu_ref[...]) + bglu_ref[...])
    s5_out = _rms(z, gs5_ref[...])

    o1 = D_S5
    o2 = o1 + RET_HEADS * RET_DK
    o3 = o2 + RET_HEADS * RET_DK
    o4 = o3 + D_RET
    q_all = _dot(h, win_ref[:, o1:o2])
    k_all = _dot(h, win_ref[:, o2:o3])
    v_all = _dot(h, win_ref[:, o3:o4]).astype(BF16)
    g_all = _dot(h, win_ref[:, o4:o4 + D_RET])
    cos2 = cos_ref[...]
    sin2 = sin_ref[...]
    ri = lax.broadcasted_iota(jnp.int32, (tb, tb), 0)
    ci = lax.broadcasted_iota(jnp.int32, (tb, tb), 1)
    same = (ri // chunk) == (ci // chunk)
    earlier = (ci // chunk) < (ri // chunk)
    dist = (ri - ci).astype(F32)
    pos = lax.broadcasted_iota(jnp.int32, (tb, 1), 0).astype(F32)
    gret = gret_ref[...]
    for hd in range(RET_HEADS):
        lg = math.log(1.0 - 2.0 ** (-5.0 - hd))
        dmat = jnp.where(same, jnp.exp(lg * jnp.abs(dist)), jnp.where(earlier, jnp.exp(lg * dist), 0.0))
        qdec = jnp.exp(lg * (pos + 1.0)) * (RET_DK ** -0.5)
        kdec = jnp.exp(lg * (tb - 1.0 - pos))
        gblk = math.exp(lg * tb)
        cols = slice(hd * RET_DK, (hd + 1) * RET_DK)
        for b in range(nb):
            rows = slice(b * tb, (b + 1) * tb)
            qh = q_all[rows, cols]
            kh = k_all[rows, cols]
            qh = qh * cos2 + pltpu.roll(qh, RET_DK // 2, axis=1) * sin2
            kh = kh * cos2 + pltpu.roll(kh, RET_DK // 2, axis=1) * sin2
            vh = v_all[rows, cols]
            s_prev = ret_ref[b, hd]
            sc = _dot_nt((qh * (RET_DK ** -0.5)).astype(BF16), kh.astype(BF16)) * dmat
            o = _dot(sc.astype(BF16), vh) + _dot((qh * qdec).astype(BF16), s_prev.astype(BF16))
            ret_ref[b, hd] = gblk * s_prev + _dot_tn((kh * kdec).astype(BF16), vh)
            o = _rms(o, gret[:, cols])
            gh = g_all[rows, cols]
            reto_ref[rows, cols] = gh * _sigmoid(gh) * o

    mix = _dot(s5_out.astype(BF16), wout_ref[0:D_S5, :]) + _dot(reto_ref[...].astype(BF16), wout_ref[D_S5:, :])
    y_ref[...] = (x + _rms(mix, gpost_ref[...])).reshape(nb, tb, d)


def _s5_discretize(lam_re, lam_im, log_step, b_re, b_im, c_re, c_im):
    dt = jnp.exp(log_step)[:, None]
    ar = lam_re * dt
    ai = lam_im * dt
    mag = jnp.exp(ar)
    abar_re = mag * jnp.cos(ai)
    abar_im = mag * jnp.sin(ai)
    den = lam_re * lam_re + lam_im * lam_im
    nr = abar_re - 1.0
    ni = abar_im
    f_re = (nr * lam_re + ni * lam_im) / den
    f_im = (ni * lam_re - nr * lam_im) / den
    bb_re = f_re[..., None] * b_re - f_im[..., None] * b_im
    bb_im = f_re[..., None] * b_im + f_im[..., None] * b_re

    gpt = MXU // S5_STATE
    gps = LANE // S5_GROUP
    n_tiles = S5_LANES // MXU

    def pack_b(bb):
        bt = bb.transpose(0, 2, 1).reshape(n_tiles, gpt, S5_GROUP, S5_STATE)
        n_idx = jnp.arange(n_tiles)[:, None, None]
        gl = jnp.arange(gps)[None, :, None]
        gi = jnp.arange(gpt)[None, None, :]
        sel = (gl == (n_idx % (gps // gpt)) * gpt + gi).astype(F32)
        w = jnp.einsum('nlg,nghp->nlhgp', sel, bt)
        return w.reshape(n_tiles, LANE, MXU).astype(BF16)

    gpo = MXU // S5_GROUP
    n_out = D_S5 // MXU

    def pack_c(c):
        ct = c.reshape(n_out, gpo, S5_GROUP, S5_STATE)
        eye = jnp.eye(gpo, dtype=F32)
        w = jnp.einsum('lg,nghp->nlpgh', eye, ct)
        return w.reshape(n_out, gpo * S5_STATE, MXU).astype(BF16)

    return (abar_re.reshape(1, S5_LANES), abar_im.reshape(1, S5_LANES),
            pack_b(bb_re), pack_b(bb_im), pack_c(c_re), pack_c(-c_im))


def _rope_tables(pos0, length):
    half = RET_DK // 2
    inv = ROPE_BASE ** (-jnp.arange(half, dtype=F32) / half)
    ang = (pos0 + jnp.arange(length)).astype(F32)[:, None] * inv[None, :]
    cos = jnp.cos(ang)
    sin = jnp.sin(ang)
    return jnp.concatenate([cos, cos], axis=1), jnp.concatenate([-sin, sin], axis=1)


def _mixer(x, tables, s5re0, s5im0, ret0, lw, *, tb):
    nb, length, d = x.shape
    chunk = CHUNK if length >= CHUNK else length
    assert length % tb == 0 and tb % chunk == 0
    m = nb * tb
    cos2, sin2 = tables
    consts = (lw['mix_norm_pre'], lw['mix_norm_post'], lw['w_in'], lw['a_re'], lw['a_im'],
              lw['wb_re'], lw['wb_im'], lw['wc_re'], lw['wc_im'], lw['s5_d'], lw['s5_w_glu'],
              lw['s5_b_glu'], lw['s5_out_norm'], lw['ret_out_norm'], lw['w_out'])
    state_shapes = (s5re0.shape, s5im0.shape, ret0.shape)
    in_specs = ([pl.BlockSpec((nb, tb, d), lambda j: (0, j, 0)),
                 pl.BlockSpec((tb, RET_DK), lambda j: (j, 0)),
                 pl.BlockSpec((tb, RET_DK), lambda j: (j, 0))]
                + [_const_spec(s) for s in state_shapes]
                + [_const_spec(c.shape) for c in consts])
    out_specs = [pl.BlockSpec((nb, tb, d), lambda j: (0, j, 0))] + [
        pl.BlockSpec(s, functools.partial(lambda n, j: (0,) * n, len(s))) for s in state_shapes]
    out_shape = [jax.ShapeDtypeStruct(x.shape, F32)] + [jax.ShapeDtypeStruct(s, F32) for s in state_shapes]
    return pl.pallas_call(
        functools.partial(_mixer_kernel, chunk=chunk),
        grid=(length // tb,),
        in_specs=in_specs,
        out_specs=out_specs,
        out_shape=out_shape,
        scratch_shapes=[pltpu.VMEM((N_LANE_TILES, m, LANE), F32),
                        pltpu.VMEM((N_LANE_TILES, m, LANE), F32),
                        pltpu.VMEM((m, D_RET), F32)],
        compiler_params=pltpu.CompilerParams(dimension_semantics=("arbitrary",),
                                             vmem_limit_bytes=VMEM_LIMIT),
        name="mixer",
    )(x, cos2, sin2, s5re0, s5im0, ret0, *consts)


def _cross_kernel(x_ref, k_ref, v_ref, gpre_ref, gpost_ref, wq_ref, wo_ref, y_ref, o_ref):
    _, tq, d = x_ref.shape
    dh = d // X_HEADS
    x = x_ref[0]
    h = _rms(x, gpre_ref[...]).astype(BF16)
    q = _dot(h, wq_ref[...]) * (dh ** -0.5)
    for hd in range(X_HEADS):
        cols = slice(hd * dh, (hd + 1) * dh)
        kh = k_ref[0, :, cols].astype(BF16)
        vh = v_ref[0, :, cols].astype(BF16)
        s = _dot_nt(q[:, cols].astype(BF16), kh)
        p = jnp.exp(s - jnp.max(s, axis=-1, keepdims=True))
        l = jnp.sum(p, axis=-1, keepdims=True)
        o_ref[:, cols] = _dot(p.astype(BF16), vh) / l
    out = _dot(o_ref[...].astype(BF16), wo_ref[...])
    y_ref[0] = x + _rms(out, gpost_ref[...])


def _cross(x, mem_k, mem_v, lw, *, tq):
    nb, length, d = x.shape
    n_mem = mem_k.shape[1]
    assert length % tq == 0
    consts = (lw['xattn_norm_pre'], lw['xattn_norm_post'], lw['w_cq'], lw['w_co'])
    return pl.pallas_call(
        _cross_kernel,
        grid=(nb, length // tq),
        in_specs=[pl.BlockSpec((1, tq, d), lambda b, j: (b, j, 0)),
                  pl.BlockSpec((1, n_mem, d), lambda b, j: (b, 0, 0)),
                  pl.BlockSpec((1, n_mem, d), lambda b, j: (b, 0, 0))]
                 + [_const_spec(c.shape) for c in consts],
        out_specs=pl.BlockSpec((1, tq, d), lambda b, j: (b, j, 0)),
        out_shape=jax.ShapeDtypeStruct(x.shape, F32),
        scratch_shapes=[pltpu.VMEM((tq, d), F32)],
        compiler_params=pltpu.CompilerParams(dimension_semantics=("arbitrary", "arbitrary"),
                                             vmem_limit_bytes=VMEM_LIMIT),
        name="cross",
    )(x, mem_k, mem_v, *consts)


def _ffn_kernel(x_ref, gpre_ref, gpost_ref, wg_ref, wu_ref, wd_ref, y_ref):
    x = x_ref[...]
    h = _rms(x, gpre_ref[...]).astype(BF16)
    g = _dot(h, wg_ref[...])
    a = (g * _sigmoid(g) * _dot(h, wu_ref[...])).astype(BF16)
    f = _dot(a, wd_ref[...])
    y_ref[...] = x + _rms(f, gpost_ref[...])


def _ffn(x, lw, *, tm):
    shape = x.shape
    d = shape[-1]
    x2 = x.reshape(-1, d)
    rows = x2.shape[0]
    assert rows % tm == 0
    consts = (lw['ffn_norm_pre'], lw['ffn_norm_post'], lw['w_gate'], lw['w_up'], lw['w_down'])
    y = pl.pallas_call(
        _ffn_kernel,
        grid=(rows // tm,),
        in_specs=[pl.BlockSpec((tm, d), lambda j: (j, 0))] + [_const_spec(c.shape) for c in consts],
        out_specs=pl.BlockSpec((tm, d), lambda j: (j, 0)),
        out_shape=jax.ShapeDtypeStruct(x2.shape, F32),
        compiler_params=pltpu.CompilerParams(dimension_semantics=("arbitrary",),
                                             vmem_limit_bytes=VMEM_LIMIT),
        name="ffn",
    )(x2, *consts)
    return y.reshape(shape)


def _memkv_kernel(mem_ref, g_ref, wk_ref, wv_ref, k_ref, v_ref):
    m = _rms(mem_ref[0], g_ref[0]).astype(BF16)
    k_ref[0, 0] = _dot(m, wk_ref[0])
    v_ref[0, 0] = _dot(m, wv_ref[0])


def _memory_kv(mem, g_mem, w_ck, w_cv):
    depth = g_mem.shape[0]
    nb, n_mem, d = mem.shape
    out = jax.ShapeDtypeStruct((depth, nb, n_mem, d), F32)
    return pl.pallas_call(
        _memkv_kernel,
        grid=(depth, nb),
        in_specs=[pl.BlockSpec((1, n_mem, d), lambda i, b: (b, 0, 0)),
                  pl.BlockSpec((1, 1, d), lambda i, b: (i, 0, 0)),
                  pl.BlockSpec((1, d, d), lambda i, b: (i, 0, 0)),
                  pl.BlockSpec((1, d, d), lambda i, b: (i, 0, 0))],
        out_specs=[pl.BlockSpec((1, 1, n_mem, d), lambda i, b: (i, b, 0, 0))] * 2,
        out_shape=[out, out],
        compiler_params=pltpu.CompilerParams(dimension_semantics=("arbitrary", "arbitrary"),
                                             vmem_limit_bytes=VMEM_LIMIT),
        name="memory_kv",
    )(mem, g_mem.reshape(depth, 1, d), w_ck, w_cv)


def _layer_weights(p, i):
    d = p['w_in'].shape[1]
    row = lambda a: a.reshape(1, -1).astype(F32)
    a_re, a_im, wb_re, wb_im, wc_re, wc_im = _s5_discretize(
        p['s5_lambda_re'][i], p['s5_lambda_im'][i], p['s5_log_step'][i],
        p['s5_b_re'][i], p['s5_b_im'][i], p['s5_c_re'][i], p['s5_c_im'][i])
    del d
    return {
        'mix_norm_pre': row(p['mix_norm_pre'][i]), 'mix_norm_post': row(p['mix_norm_post'][i]),
        'w_in': p['w_in'][i].astype(BF16),
        'a_re': a_re, 'a_im': a_im, 'wb_re': wb_re, 'wb_im': wb_im, 'wc_re': wc_re, 'wc_im': wc_im,
        's5_d': row(p['s5_d'][i]), 's5_w_glu': p['s5_w_glu'][i].astype(BF16),
        's5_b_glu': row(p['s5_b_glu'][i]), 's5_out_norm': row(p['s5_out_norm'][i]),
        'ret_out_norm': row(p['ret_out_norm'][i]), 'w_out': p['w_out'][i].astype(BF16),
        'xattn_norm_pre': row(p['xattn_norm_pre'][i]), 'xattn_norm_post': row(p['xattn_norm_post'][i]),
        'w_cq': p['w_cq'][i].astype(BF16), 'w_co': p['w_co'][i].astype(BF16),
        'ffn_norm_pre': row(p['ffn_norm_pre'][i]), 'ffn_norm_post': row(p['ffn_norm_post'][i]),
        'w_gate': p['w_gate'][i].astype(BF16), 'w_up': p['w_up'][i].astype(BF16),
        'w_down': p['w_down'][i].astype(BF16),
    }


def _block_rows(length, target):
    return target if length >= target and length % target == 0 else length


def _trunk(x, pos0, mem_k, mem_v, s5_re0, s5_im0, ret0, weights):
    nb, length, d = x.shape
    depth = len(weights)
    tables = _rope_tables(pos0, length)
    tb = _block_rows(length, 128)
    tq = _block_rows(length, 512)
    tm = _block_rows(nb * length, 512)
    s5r_all, s5i_all, ret_all = [], [], []
    for i in range(depth):
        lw = weights[i]
        x, s5r, s5i, s_ret = _mixer(x, tables, s5_re0[i].reshape(nb, S5_LANES), s5_im0[i].reshape(nb, S5_LANES),
                                    ret0[i], lw, tb=tb)
        x = _cross(x, mem_k[i], mem_v[i], lw, tq=tq)
        x = _ffn(x, lw, tm=tm)
        s5r_all.append(s5r.reshape(nb, S5_GROUPS, S5_STATE))
        s5i_all.append(s5i.reshape(nb, S5_GROUPS, S5_STATE))
        ret_all.append(s_ret)
    return x, jnp.stack(s5r_all), jnp.stack(s5i_all), jnp.stack(ret_all)


def kernel(x_prompt, x_sample, mem_prompt, state_s5_re, state_s5_im, state_ret, cache_mem_k, cache_mem_v,
           mix_norm_pre, mix_norm_post, w_in, s5_lambda_re, s5_lambda_im, s5_log_step,
           s5_b_re, s5_b_im, s5_c_re, s5_c_im, s5_d, s5_w_glu, s5_b_glu, s5_out_norm, ret_out_norm,
           w_out, xattn_norm_pre, xattn_norm_post, mem_norm, w_cq, w_ck, w_cv, w_co,
           ffn_norm_pre, ffn_norm_post, w_gate, w_up, w_down):
    p = {
        'mix_norm_pre': mix_norm_pre, 'mix_norm_post': mix_norm_post, 'w_in': w_in,
        's5_lambda_re': s5_lambda_re, 's5_lambda_im': s5_lambda_im, 's5_log_step': s5_log_step,
        's5_b_re': s5_b_re, 's5_b_im': s5_b_im, 's5_c_re': s5_c_re, 's5_c_im': s5_c_im, 's5_d': s5_d,
        's5_w_glu': s5_w_glu, 's5_b_glu': s5_b_glu, 's5_out_norm': s5_out_norm,
        'ret_out_norm': ret_out_norm, 'w_out': w_out,
        'xattn_norm_pre': xattn_norm_pre, 'xattn_norm_post': xattn_norm_post,
        'w_cq': w_cq, 'w_co': w_co,
        'ffn_norm_pre': ffn_norm_pre, 'ffn_norm_post': ffn_norm_post,
        'w_gate': w_gate, 'w_up': w_up, 'w_down': w_down,
    }
    depth = w_in.shape[0]
    nbp, _, d = x_prompt.shape
    nbs = x_sample.shape[0]
    n_mem = mem_prompt.shape[1]
    weights = [_layer_weights(p, i) for i in range(depth)]

    mk, mv = _memory_kv(mem_prompt, mem_norm, w_ck.astype(BF16), w_cv.astype(BF16))
    zeros_s5 = jnp.zeros((depth, nbp, S5_GROUPS, S5_STATE), F32)
    zeros_ret = jnp.zeros((depth, nbp, RET_HEADS, RET_DK, RET_DV), F32)
    y_prompt, s5r_p, s5i_p, ret_p = _trunk(x_prompt, 0, mk, mv, zeros_s5, zeros_s5, zeros_ret, weights)

    ck = cache_mem_k.reshape(depth, nbs, n_mem, d)
    cv = cache_mem_v.reshape(depth, nbs, n_mem, d)
    y_sample, s5r_s, s5i_s, ret_s = _trunk(x_sample, PAST_LEN, ck, cv, state_s5_re, state_s5_im, state_ret, weights)

    kv_shape = (depth, nbp, n_mem, X_HEADS, d // X_HEADS)
    return (y_prompt, y_sample, s5r_p, s5i_p, ret_p, mk.reshape(kv_shape), mv.reshape(kv_shape),
            s5r_s, s5i_s, ret_s)
```

```python
import functools
import math

import jax
import jax.numpy as jnp
from jax import lax
from jax.experimental import pallas as pl
from jax.experimental.pallas import tpu as pltpu

F32 = jnp.float32
BF16 = jnp.bfloat16

PAST_LEN = 1024
CHUNK = 64
D_S5 = 512
S5_GROUP = 16
S5_GROUPS = D_S5 // S5_GROUP
S5_STATE = 64
S5_LANES = S5_GROUPS * S5_STATE
RET_HEADS = 4
RET_DK = 128
RET_DV = 128
D_RET = RET_HEADS * RET_DV
X_HEADS = 4
ROPE_BASE = 10000.0
EPS = 1e-6

LANE = 128
MXU = 256
N_LANE_TILES = S5_LANES // LANE
SCAN_TILES = 8
SCAN_SEQS = 4
VMEM_LIMIT = 56 * 1024 * 1024


def _const_spec(shape):
    zeros = (0,) * len(shape)
    return pl.BlockSpec(shape, lambda *_: zeros, pipeline_mode=pl.Buffered(1))


def _rms(x, g):
    ms = jnp.mean(x * x, axis=-1, keepdims=True)
    return x * lax.rsqrt(ms + EPS) * g


def _dot(a, b):
    return jnp.dot(a, b, preferred_element_type=F32)


def _dot_nt(a, b):
    return lax.dot_general(a, b, (((1,), (1,)), ((), ())), preferred_element_type=F32)


def _dot_tn(a, b):
    return lax.dot_general(a, b, (((0,), (0,)), ((), ())), preferred_element_type=F32)


def _gelu_tanh(x):
    c = math.sqrt(2.0 / math.pi)
    return 0.5 * x * (1.0 + jnp.tanh(c * (x + 0.044715 * (x * x * x))))


def _sigmoid(x):
    return 1.0 / (1.0 + jnp.exp(-x))


def _mixer_kernel(x_ref, cos_ref, sin_ref, s5x0_ref, ret0_ref,
                  gpre_ref, gpost_ref, win_ref, ar_ref, ais_ref, wb_ref,
                  wcre_ref, wcim_ref, dskip_ref, wglu_ref, bglu_ref, gs5_ref, gret_ref, wout_ref,
                  y_ref, s5x_ref, ret_ref,
                  ut_ref, bu_ref, xre_ref, xim_ref, yt_ref, reto_ref, *, chunk):
    nb, tb, d = x_ref.shape
    assert nb == SCAN_SEQS
    m = nb * tb
    pairs = tb // 2

    @pl.when(pl.program_id(1) == 0)
    def _():
        s5x_ref[...] = s5x0_ref[...]
        ret_ref[...] = ret0_ref[...]

    x = x_ref[...].reshape(m, d)
    h = _rms(x, gpre_ref[...]).astype(BF16)

    u = _dot(h, win_ref[:, 0:D_S5])
    for lt in range(D_S5 // LANE):
        for b in range(nb):
            ut_ref.at[lt][pl.ds(b, tb, stride=nb), :] = u[b * tb:(b + 1) * tb, lt * LANE:(lt + 1) * LANE]

    lo = lax.broadcasted_iota(jnp.int32, (1, 2 * nb, LANE), 1) < nb
    for ks in range(D_S5 // LANE):
        t3 = ut_ref[ks].reshape(pairs, 2 * nb, LANE)
        r3 = pltpu.roll(t3, nb, axis=1)
        even = jnp.concatenate([jnp.where(lo, t3, 0.0), jnp.where(lo, 0.0, r3)], axis=-1)
        odd = jnp.concatenate([jnp.where(lo, r3, 0.0), jnp.where(lo, 0.0, t3)], axis=-1)
        for parity, lhs3 in ((0, even), (1, odd)):
            lhs = lhs3.reshape(pairs * 2 * nb, 2 * LANE).astype(BF16)
            for n in range(2 * ks, 2 * ks + 2):
                r = _dot(lhs, wb_ref[n]).reshape(pairs, 2 * nb, MXU)
                bu_ref[2 * n, :, parity] = r[:, :, :LANE]
                bu_ref[2 * n + 1, :, parity] = r[:, :, LANE:]

    lo2 = lax.broadcasted_iota(jnp.int32, (2 * nb, LANE), 0) < nb
    for c0 in range(0, N_LANE_TILES, SCAN_TILES):
        tiles = range(c0, c0 + SCAN_TILES)
        init = []
        for i in tiles:
            x0 = s5x_ref[0, i]
            init += [x0, pltpu.roll(x0, nb, axis=0)]

        def pair_step(k, carry):
            out = []
            row0 = pl.multiple_of(k * 2 * nb, 2 * nb)
            for idx, i in enumerate(tiles):
                xc, xs = carry[2 * idx], carry[2 * idx + 1]
                ar = ar_ref[i]
                ais = ais_ref[i]
                xe = ar * xc + (ais * xs + bu_ref[i, k, 0])
                xes = pltpu.roll(xe, nb, axis=0)
                xo = ar * xe + (ais * xes + bu_ref[i, k, 1])
                xos = pltpu.roll(xo, nb, axis=0)
                xre_ref[i, pl.ds(row0, 2 * nb), :] = jnp.where(lo2, xe, xos)
                xim_ref[i, pl.ds(row0, 2 * nb), :] = jnp.where(lo2, xes, xo)
                out += [xo, xos]
            return tuple(out)

        fin = lax.fori_loop(0, pairs, pair_step, tuple(init), unroll=2)
        for idx, i in enumerate(tiles):
            s5x_ref[0, i] = fin[2 * idx]

    tiles_per_out = (MXU // S5_GROUP) * S5_STATE // LANE
    ut = jnp.concatenate([ut_ref[lt] for lt in range(D_S5 // LANE)], axis=1)
    for n in range(D_S5 // MXU):
        xr = jnp.concatenate([xre_ref[n * tiles_per_out + i] for i in range(tiles_per_out)], axis=1)
        xi = jnp.concatenate([xim_ref[n * tiles_per_out + i] for i in range(tiles_per_out)], axis=1)
        yn = _dot(xr.astype(BF16), wcre_ref[n]) + _dot(xi.astype(BF16), wcim_ref[n])
        yn = yn + dskip_ref[:, n * MXU:(n + 1) * MXU] * ut[:, n * MXU:(n + 1) * MXU]
        yt_ref[2 * n] = yn[:, :LANE]
        yt_ref[2 * n + 1] = yn[:, LANE:]
    y = jnp.concatenate(
        [jnp.concatenate([yt_ref.at[lt][pl.ds(b, tb, stride=nb), :] for lt in range(D_S5 // LANE)], axis=1)
         for b in range(nb)], axis=0)
    z = _gelu_tanh(y)
    z = z * _sigmoid(_dot(z.astype(BF16), wglu_ref[...]) + bglu_ref[...])
    s5_out = _rms(z, gs5_ref[...])

    o1 = D_S5
    o2 = o1 + RET_HEADS * RET_DK
    o3 = o2 + RET_HEADS * RET_DK
    o4 = o3 + D_RET
    q_all = _dot(h, win_ref[:, o1:o2])
    k_all = _dot(h, win_ref[:, o2:o3])
    v_all = _dot(h, win_ref[:, o3:o4]).astype(BF16)
    g_all = _dot(h, win_ref[:, o4:o4 + D_RET])
    cos2 = cos_ref[...]
    sin2 = sin_ref[...]
    ri = lax.broadcasted_iota(jnp.int32, (tb, tb), 0)
    ci = lax.broadcasted_iota(jnp.int32, (tb, tb), 1)
    same = (ri // chunk) == (ci // chunk)
    earlier = (ci // chunk) < (ri // chunk)
    dist = (ri - ci).astype(F32)
    pos = lax.broadcasted_iota(jnp.int32, (tb, 1), 0).astype(F32)
    gret = gret_ref[...]
    for hd in range(RET_HEADS):
        lg = math.log(1.0 - 2.0 ** (-5.0 - hd))
        dmat = jnp.where(same, jnp.exp(lg * jnp.abs(dist)), jnp.where(earlier, jnp.exp(lg * dist), 0.0))
        qdec = jnp.exp(lg * (pos + 1.0)) * (RET_DK ** -0.5)
        kdec = jnp.exp(lg * (tb - 1.0 - pos))
        gblk = math.exp(lg * tb)
        cols = slice(hd * RET_DK, (hd + 1) * RET_DK)
        for b in range(nb):
            rows = slice(b * tb, (b + 1) * tb)
            qh = q_all[rows, cols]
            kh = k_all[rows, cols]
            qh = qh * cos2 + pltpu.roll(qh, RET_DK // 2, axis=1) * sin2
            kh = kh * cos2 + pltpu.roll(kh, RET_DK // 2, axis=1) * sin2
            vh = v_all[rows, cols]
            s_prev = ret_ref[b, hd]
            sc = _dot_nt((qh * (RET_DK ** -0.5)).astype(BF16), kh.astype(BF16)) * dmat
            o = _dot(sc.astype(BF16), vh) + _dot((qh * qdec).astype(BF16), s_prev.astype(BF16))
            ret_ref[b, hd] = gblk * s_prev + _dot_tn((kh * kdec).astype(BF16), vh)
            o = _rms(o, gret[:, cols])
            gh = g_all[rows, cols]
            reto_ref[rows, cols] = gh * _sigmoid(gh) * o

    mix = _dot(s5_out.astype(BF16), wout_ref[0:D_S5, :]) + _dot(reto_ref[...].astype(BF16), wout_ref[D_S5:, :])
    y_ref[...] = (x + _rms(mix, gpost_ref[...])).reshape(nb, tb, d)


def _s5_discretize(lam_re, lam_im, log_step, b_re, b_im, c_re, c_im):
    dt = jnp.exp(log_step)[:, None]
    ar = lam_re * dt
    ai = lam_im * dt
    mag = jnp.exp(ar)
    abar_re = mag * jnp.cos(ai)
    abar_im = mag * jnp.sin(ai)
    den = lam_re * lam_re + lam_im * lam_im
    nr = abar_re - 1.0
    ni = abar_im
    f_re = (nr * lam_re + ni * lam_im) / den
    f_im = (ni * lam_re - nr * lam_im) / den
    bb_re = f_re[..., None] * b_re - f_im[..., None] * b_im
    bb_im = f_re[..., None] * b_im + f_im[..., None] * b_re

    gpt = MXU // S5_STATE
    gps = LANE // S5_GROUP
    n_tiles = S5_LANES // MXU

    def pack_b(bb):
        bt = bb.transpose(0, 2, 1).reshape(n_tiles, gpt, S5_GROUP, S5_STATE)
        n_idx = jnp.arange(n_tiles)[:, None, None]
        gl = jnp.arange(gps)[None, :, None]
        gi = jnp.arange(gpt)[None, None, :]
        sel = (gl == (n_idx % (gps // gpt)) * gpt + gi).astype(F32)
        w = jnp.einsum('nlg,nghp->nlhgp', sel, bt)
        return w.reshape(n_tiles, LANE, MXU).astype(BF16)

    gpo = MXU // S5_GROUP
    n_out = D_S5 // MXU

    def pack_c(c):
        ct = c.reshape(n_out, gpo, S5_GROUP, S5_STATE)
        eye = jnp.eye(gpo, dtype=F32)
        w = jnp.einsum('lg,nghp->nlpgh', eye, ct)
        return w.reshape(n_out, gpo * S5_STATE, MXU).astype(BF16)

    def scan_rows(lo_half, hi_half):
        lo = jnp.broadcast_to(lo_half.reshape(N_LANE_TILES, 1, LANE), (N_LANE_TILES, SCAN_SEQS, LANE))
        hi = jnp.broadcast_to(hi_half.reshape(N_LANE_TILES, 1, LANE), (N_LANE_TILES, SCAN_SEQS, LANE))
        return jnp.concatenate([lo, hi], axis=1)

    wb = jnp.concatenate([pack_b(bb_re), pack_b(bb_im)], axis=1)
    return (scan_rows(abar_re, abar_re), scan_rows(-abar_im, abar_im), wb, pack_c(c_re), pack_c(-c_im))


def _rope_tables(pos0, length):
    half = RET_DK // 2
    inv = ROPE_BASE ** (-jnp.arange(half, dtype=F32) / half)
    ang = (pos0 + jnp.arange(length)).astype(F32)[:, None] * inv[None, :]
    cos = jnp.cos(ang)
    sin = jnp.sin(ang)
    return jnp.concatenate([cos, cos], axis=1), jnp.concatenate([-sin, sin], axis=1)


def _pack_s5_state(re, im):
    def tiles(a):
        return a.reshape(-1, SCAN_SEQS, N_LANE_TILES, LANE).transpose(0, 2, 1, 3)
    return jnp.concatenate([tiles(re), tiles(im)], axis=2)


def _unpack_s5_state(tiles):
    def seqs(a):
        return a.transpose(0, 2, 1, 3).reshape(-1, S5_GROUPS, S5_STATE)
    return seqs(tiles[:, :, :SCAN_SEQS]), seqs(tiles[:, :, SCAN_SEQS:])


def _mixer(x, tables, s5x0, ret0, lw, *, tb):
    nb, length, d = x.shape
    chunk = CHUNK if length >= CHUNK else length
    assert length % tb == 0 and tb % chunk == 0 and tb % 2 == 0 and nb % SCAN_SEQS == 0
    m = SCAN_SEQS * tb
    cos2, sin2 = tables
    consts = (lw['mix_norm_pre'], lw['mix_norm_post'], lw['w_in'], lw['a_rows'], lw['ais_rows'],
              lw['wb'], lw['wc_re'], lw['wc_im'], lw['s5_d'], lw['s5_w_glu'],
              lw['s5_b_glu'], lw['s5_out_norm'], lw['ret_out_norm'], lw['w_out'])
    x_spec = pl.BlockSpec((SCAN_SEQS, tb, d), lambda g, j: (g, j, 0))
    s5_spec = pl.BlockSpec((1,) + s5x0.shape[1:], lambda g, j: (g, 0, 0, 0))
    ret_spec = pl.BlockSpec((SCAN_SEQS,) + ret0.shape[1:], lambda g, j: (g, 0, 0, 0))
    in_specs = ([x_spec,
                 pl.BlockSpec((tb, RET_DK), lambda g, j: (j, 0)),
                 pl.BlockSpec((tb, RET_DK), lambda g, j: (j, 0)),
                 s5_spec, ret_spec]
                + [_const_spec(c.shape) for c in consts])
    return pl.pallas_call(
        functools.partial(_mixer_kernel, chunk=chunk),
        grid=(nb // SCAN_SEQS, length // tb),
        in_specs=in_specs,
        out_specs=[x_spec, s5_spec, ret_spec],
        out_shape=[jax.ShapeDtypeStruct(x.shape, F32), jax.ShapeDtypeStruct(s5x0.shape, F32),
                   jax.ShapeDtypeStruct(ret0.shape, F32)],
        scratch_shapes=[pltpu.VMEM((D_S5 // LANE, m, LANE), F32),
                        pltpu.VMEM((N_LANE_TILES, tb // 2, 2, 2 * SCAN_SEQS, LANE), F32),
                        pltpu.VMEM((N_LANE_TILES, m, LANE), F32),
                        pltpu.VMEM((N_LANE_TILES, m, LANE), F32),
                        pltpu.VMEM((D_S5 // LANE, m, LANE), F32),
                        pltpu.VMEM((m, D_RET), F32)],
        compiler_params=pltpu.CompilerParams(dimension_semantics=("arbitrary", "arbitrary"),
                                             vmem_limit_bytes=VMEM_LIMIT),
        name="mixer",
    )(x, cos2, sin2, s5x0, ret0, *consts)


def _cross_kernel(x_ref, k_ref, v_ref, gpre_ref, gpost_ref, wq_ref, wo_ref, y_ref, o_ref):
    _, tq, d = x_ref.shape
    dh = d // X_HEADS
    x = x_ref[0]
    h = _rms(x, gpre_ref[...]).astype(BF16)
    q = _dot(h, wq_ref[...]) * (dh ** -0.5)
    for hd in range(X_HEADS):
        cols = slice(hd * dh, (hd + 1) * dh)
        kh = k_ref[0, :, cols].astype(BF16)
        vh = v_ref[0, :, cols].astype(BF16)
        s = _dot_nt(q[:, cols].astype(BF16), kh)
        p = jnp.exp(s - jnp.max(s, axis=-1, keepdims=True))
        l = jnp.sum(p, axis=-1, keepdims=True)
        o_ref[:, cols] = _dot(p.astype(BF16), vh) / l
    out = _dot(o_ref[...].astype(BF16), wo_ref[...])
    y_ref[0] = x + _rms(out, gpost_ref[...])


def _cross(x, mem_k, mem_v, lw, *, tq):
    nb, length, d = x.shape
    n_mem = mem_k.shape[1]
    assert length % tq == 0
    consts = (lw['xattn_norm_pre'], lw['xattn_norm_post'], lw['w_cq'], lw['w_co'])
    return pl.pallas_call(
        _cross_kernel,
        grid=(nb, length // tq),
        in_specs=[pl.BlockSpec((1, tq, d), lambda b, j: (b, j, 0)),
                  pl.BlockSpec((1, n_mem, d), lambda b, j: (b, 0, 0)),
                  pl.BlockSpec((1, n_mem, d), lambda b, j: (b, 0, 0))]
                 + [_const_spec(c.shape) for c in consts],
        out_specs=pl.BlockSpec((1, tq, d), lambda b, j: (b, j, 0)),
        out_shape=jax.ShapeDtypeStruct(x.shape, F32),
        scratch_shapes=[pltpu.VMEM((tq, d), F32)],
        compiler_params=pltpu.CompilerParams(dimension_semantics=("arbitrary", "arbitrary"),
                                             vmem_limit_bytes=VMEM_LIMIT),
        name="cross",
    )(x, mem_k, mem_v, *consts)


def _ffn_kernel(x_ref, gpre_ref, gpost_ref, wg_ref, wu_ref, wd_ref, y_ref):
    x = x_ref[...]
    h = _rms(x, gpre_ref[...]).astype(BF16)
    g = _dot(h, wg_ref[...])
    a = (g * _sigmoid(g) * _dot(h, wu_ref[...])).astype(BF16)
    f = _dot(a, wd_ref[...])
    y_ref[...] = x + _rms(f, gpost_ref[...])


def _ffn(x, lw, *, tm):
    shape = x.shape
    d = shape[-1]
    x2 = x.reshape(-1, d)
    rows = x2.shape[0]
    assert rows % tm == 0
    consts = (lw['ffn_norm_pre'], lw['ffn_norm_post'], lw['w_gate'], lw['w_up'], lw['w_down'])
    y = pl.pallas_call(
        _ffn_kernel,
        grid=(rows // tm,),
        in_specs=[pl.BlockSpec((tm, d), lambda j: (j, 0))] + [_const_spec(c.shape) for c in consts],
        out_specs=pl.BlockSpec((tm, d), lambda j: (j, 0)),
        out_shape=jax.ShapeDtypeStruct(x2.shape, F32),
        compiler_params=pltpu.CompilerParams(dimension_semantics=("arbitrary",),
                                             vmem_limit_bytes=VMEM_LIMIT),
        name="ffn",
    )(x2, *consts)
    return y.reshape(shape)


def _memkv_kernel(mem_ref, g_ref, wk_ref, wv_ref, k_ref, v_ref):
    m = _rms(mem_ref[0], g_ref[0]).astype(BF16)
    k_ref[0, 0] = _dot(m, wk_ref[0])
    v_ref[0, 0] = _dot(m, wv_ref[0])


def _memory_kv(mem, g_mem, w_ck, w_cv):
    depth = g_mem.shape[0]
    nb, n_mem, d = mem.shape
    out = jax.ShapeDtypeStruct((depth, nb, n_mem, d), F32)
    return pl.pallas_call(
        _memkv_kernel,
        grid=(depth, nb),
        in_specs=[pl.BlockSpec((1, n_mem, d), lambda i, b: (b, 0, 0)),
                  pl.BlockSpec((1, 1, d), lambda i, b: (i, 0, 0)),
                  pl.BlockSpec((1, d, d), lambda i, b: (i, 0, 0)),
                  pl.BlockSpec((1, d, d), lambda i, b: (i, 0, 0))],
        out_specs=[pl.BlockSpec((1, 1, n_mem, d), lambda i, b: (i, b, 0, 0))] * 2,
        out_shape=[out, out],
        compiler_params=pltpu.CompilerParams(dimension_semantics=("arbitrary", "arbitrary"),
                                             vmem_limit_bytes=VMEM_LIMIT),
        name="memory_kv",
    )(mem, g_mem.reshape(depth, 1, d), w_ck, w_cv)


def _layer_weights(p, i):
    row = lambda a: a.reshape(1, -1).astype(F32)
    a_rows, ais_rows, wb, wc_re, wc_im = _s5_discretize(
        p['s5_lambda_re'][i], p['s5_lambda_im'][i], p['s5_log_step'][i],
        p['s5_b_re'][i], p['s5_b_im'][i], p['s5_c_re'][i], p['s5_c_im'][i])
    return {
        'mix_norm_pre': row(p['mix_norm_pre'][i]), 'mix_norm_post': row(p['mix_norm_post'][i]),
        'w_in': p['w_in'][i].astype(BF16),
        'a_rows': a_rows, 'ais_rows': ais_rows, 'wb': wb, 'wc_re': wc_re, 'wc_im': wc_im,
        's5_d': row(p['s5_d'][i]), 's5_w_glu': p['s5_w_glu'][i].astype(BF16),
        's5_b_glu': row(p['s5_b_glu'][i]), 's5_out_norm': row(p['s5_out_norm'][i]),
        'ret_out_norm': row(p['ret_out_norm'][i]), 'w_out': p['w_out'][i].astype(BF16),
        'xattn_norm_pre': row(p['xattn_norm_pre'][i]), 'xattn_norm_post': row(p['xattn_norm_post'][i]),
        'w_cq': p['w_cq'][i].astype(BF16), 'w_co': p['w_co'][i].astype(BF16),
        'ffn_norm_pre': row(p['ffn_norm_pre'][i]), 'ffn_norm_post': row(p['ffn_norm_post'][i]),
        'w_gate': p['w_gate'][i].astype(BF16), 'w_up': p['w_up'][i].astype(BF16),
        'w_down': p['w_down'][i].astype(BF16),
    }


def _block_rows(length, target):
    return target if length >= target and length % target == 0 else length


def _trunk(x, pos0, mem_k, mem_v, s5_re0, s5_im0, ret0, weights):
    nb, length, d = x.shape
    depth = len(weights)
    tables = _rope_tables(pos0, length)
    tb = _block_rows(length, 128)
    tq = _block_rows(length, 512)
    tm = _block_rows(nb * length, 512)
    s5r_all, s5i_all, ret_all = [], [], []
    for i in range(depth):
        lw = weights[i]
        x, s5x, s_ret = _mixer(x, tables, _pack_s5_state(s5_re0[i], s5_im0[i]), ret0[i], lw, tb=tb)
        s5r, s5i = _unpack_s5_state(s5x)
        x = _cross(x, mem_k[i], mem_v[i], lw, tq=tq)
        x = _ffn(x, lw, tm=tm)
        s5r_all.append(s5r)
        s5i_all.append(s5i)
        ret_all.append(s_ret)
    return x, jnp.stack(s5r_all), jnp.stack(s5i_all), jnp.stack(ret_all)


def kernel(x_prompt, x_sample, mem_prompt, state_s5_re, state_s5_im, state_ret, cache_mem_k, cache_mem_v,
           mix_norm_pre, mix_norm_post, w_in, s5_lambda_re, s5_lambda_im, s5_log_step,
           s5_b_re, s5_b_im, s5_c_re, s5_c_im, s5_d, s5_w_glu, s5_b_glu, s5_out_norm, ret_out_norm,
           w_out, xattn_norm_pre, xattn_norm_post, mem_norm, w_cq, w_ck, w_cv, w_co,
           ffn_norm_pre, ffn_norm_post, w_gate, w_up, w_down):
    p = {
        'mix_norm_pre': mix_norm_pre, 'mix_norm_post': mix_norm_post, 'w_in': w_in,
        's5_lambda_re': s5_lambda_re, 's5_lambda_im': s5_lambda_im, 's5_log_step': s5_log_step,
        's5_b_re': s5_b_re, 's5_b_im': s5_b_im, 's5_c_re': s5_c_re, 's5_c_im': s5_c_im, 's5_d': s5_d,
        's5_w_glu': s5_w_glu, 's5_b_glu': s5_b_glu, 's5_out_norm': s5_out_norm,
        'ret_out_norm': ret_out_norm, 'w_out': w_out,
        'xattn_norm_pre': xattn_norm_pre, 'xattn_norm_post': xattn_norm_post,
        'w_cq': w_cq, 'w_co': w_co,
        'ffn_norm_pre': ffn_norm_pre, 'ffn_norm_post': ffn_norm_post,
        'w_gate': w_gate, 'w_up': w_up, 'w_down': w_down,
    }
    depth = w_in.shape[0]
    nbp, _, d = x_prompt.shape
    nbs = x_sample.shape[0]
    n_mem = mem_prompt.shape[1]
    weights = [_layer_weights(p, i) for i in range(depth)]

    mk, mv = _memory_kv(mem_prompt, mem_norm, w_ck.astype(BF16), w_cv.astype(BF16))
    zeros_s5 = jnp.zeros((depth, nbp, S5_GROUPS, S5_STATE), F32)
    zeros_ret = jnp.zeros((depth, nbp, RET_HEADS, RET_DK, RET_DV), F32)
    y_prompt, s5r_p, s5i_p, ret_p = _trunk(x_prompt, 0, mk, mv, zeros_s5, zeros_s5, zeros_ret, weights)

    ck = cache_mem_k.reshape(depth, nbs, n_mem, d)
    cv = cache_mem_v.reshape(depth, nbs, n_mem, d)
    y_sample, s5r_s, s5i_s, ret_s = _trunk(x_sample, PAST_LEN, ck, cv, state_s5_re, state_s5_im, state_ret, weights)

    kv_shape = (depth, nbp, n_mem, X_HEADS, d // X_HEADS)
    return (y_prompt, y_sample, s5r_p, s5i_p, ret_p, mk.reshape(kv_shape), mv.reshape(kv_shape),
            s5r_s, s5i_s, ret_s)
```

```python
import functools
import math

import jax
import jax.numpy as jnp
from jax import lax
from jax.experimental import pallas as pl
from jax.experimental.pallas import tpu as pltpu

F32 = jnp.float32
BF16 = jnp.bfloat16

PAST_LEN = 1024
CHUNK = 64
D_S5 = 512
S5_GROUP = 16
S5_GROUPS = D_S5 // S5_GROUP
S5_STATE = 64
S5_LANES = S5_GROUPS * S5_STATE
RET_HEADS = 4
RET_DK = 128
RET_DV = 128
D_RET = RET_HEADS * RET_DV
X_HEADS = 4
ROPE_BASE = 10000.0
EPS = 1e-6

LANE = 128
MXU = 256
N_LANE_TILES = S5_LANES // LANE
SCAN_TILES = 8
SCAN_PIECE = 8
SCAN_SEQS = 4
VMEM_LIMIT = 56 * 1024 * 1024


def _layer_spec(arr, layer):
    tail = (0,) * (arr.ndim - 1)
    return pl.BlockSpec((None,) + arr.shape[1:], lambda *_: (layer,) + tail, pipeline_mode=pl.Buffered(1))


def _rms(x, g):
    ms = jnp.mean(x * x, axis=-1, keepdims=True)
    return x * lax.rsqrt(ms + EPS) * g


def _dot(a, b):
    return jnp.dot(a, b, preferred_element_type=F32)


def _dot_nt(a, b):
    return lax.dot_general(a, b, (((1,), (1,)), ((), ())), preferred_element_type=F32)


def _bdot(a, b, ca, cb):
    return lax.dot_general(a, b, (((ca,), (cb,)), ((0,), (0,))), preferred_element_type=F32)


def _gelu_tanh(x):
    c = math.sqrt(2.0 / math.pi)
    return 0.5 * x * (1.0 + jnp.tanh(c * (x + 0.044715 * (x * x * x))))


def _sigmoid(x):
    return 1.0 / (1.0 + jnp.exp(-x))


def _mixer_kernel(x_ref, cos_ref, sin_ref, s5x0_ref, ret0_ref,
                  gpre_ref, gpost_ref, win_ref, ar_ref, ais_ref, wb_ref,
                  wcre_ref, wcim_ref, dskip_ref, wglu_ref, bglu_ref, gs5_ref, gret_ref, wout_ref,
                  y_ref, s5x_ref, ret_ref,
                  ut_ref, bu_ref, xre_ref, xim_ref, yt_ref, reto_ref, *, chunk):
    nb, tb, d = x_ref.shape
    assert nb == SCAN_SEQS
    m = nb * tb
    pairs = tb // 2

    @pl.when(pl.program_id(1) == 0)
    def _():
        s5x_ref[...] = s5x0_ref[...]
        ret_ref[...] = ret0_ref[...]

    x = x_ref[...].reshape(m, d)
    h = _rms(x, gpre_ref[...]).astype(BF16)

    u = _dot(h, win_ref[:, 0:D_S5])
    for lt in range(D_S5 // LANE):
        for b in range(nb):
            ut_ref.at[lt][pl.ds(b, tb, stride=nb), :] = u[b * tb:(b + 1) * tb, lt * LANE:(lt + 1) * LANE]

    lo = lax.broadcasted_iota(jnp.int32, (1, 2 * nb, LANE), 1) < nb
    for ks in range(D_S5 // LANE):
        t3 = ut_ref[ks].reshape(pairs, 2 * nb, LANE)
        r3 = pltpu.roll(t3, nb, axis=1)
        even = jnp.concatenate([jnp.where(lo, t3, 0.0), jnp.where(lo, 0.0, r3)], axis=-1)
        odd = jnp.concatenate([jnp.where(lo, r3, 0.0), jnp.where(lo, 0.0, t3)], axis=-1)
        for parity, lhs3 in ((0, even), (1, odd)):
            lhs = lhs3.reshape(pairs * 2 * nb, 2 * LANE).astype(BF16)
            for n in range(2 * ks, 2 * ks + 2):
                r = _dot(lhs, wb_ref[n]).reshape(pairs, 2 * nb, MXU)
                bu_ref[2 * n, :, parity] = r[:, :, :LANE]
                bu_ref[2 * n + 1, :, parity] = r[:, :, LANE:]

    lo2 = lax.broadcasted_iota(jnp.int32, (2 * nb, LANE), 0) < nb

    def scan_pieces(pairs_per_piece):
        for c0 in range(0, N_LANE_TILES, SCAN_TILES):
            tiles = range(c0, c0 + SCAN_TILES)
            state = {}
            for i in tiles:
                x0 = s5x_ref[0, i]
                state[i] = (x0, pltpu.roll(x0, nb, axis=0))
            for k in range(pairs):
                for i in tiles:
                    xc, xs = state[i]
                    ar = ar_ref[i]
                    ais = ais_ref[i]
                    xe = ar * xc + (ais * xs + bu_ref[i, k, 0])
                    xes = pltpu.roll(xe, nb, axis=0)
                    xo = ar * xe + (ais * xes + bu_ref[i, k, 1])
                    xos = pltpu.roll(xo, nb, axis=0)
                    rows = slice(k * 2 * nb, (k + 1) * 2 * nb)
                    xre_ref[i, rows, :] = jnp.where(lo2, xe, xos)
                    xim_ref[i, rows, :] = jnp.where(lo2, xes, xo)
                    state[i] = (xo, xos)
                if (k + 1) % pairs_per_piece == 0:
                    yield
            for i in tiles:
                s5x_ref[0, i] = state[i][0]

    o1 = D_S5
    o2 = o1 + RET_HEADS * RET_DK
    o3 = o2 + RET_HEADS * RET_DK
    o4 = o3 + D_RET
    scan = scan_pieces(SCAN_PIECE)
    q_all = _dot(h, win_ref[:, o1:o2])
    next(scan, None)
    k_all = _dot(h, win_ref[:, o2:o3])
    next(scan, None)
    v_all = _dot(h, win_ref[:, o3:o4]).astype(BF16)
    next(scan, None)
    g_all = _dot(h, win_ref[:, o4:o4 + D_RET])
    next(scan, None)
    cos2 = cos_ref[...]
    sin2 = sin_ref[...]
    ri = lax.broadcasted_iota(jnp.int32, (tb, tb), 0)
    ci = lax.broadcasted_iota(jnp.int32, (tb, tb), 1)
    same = (ri // chunk) == (ci // chunk)
    earlier = (ci // chunk) < (ri // chunk)
    dist = (ri - ci).astype(F32)
    pos = lax.broadcasted_iota(jnp.int32, (tb, 1), 0).astype(F32)
    gret = gret_ref[...]
    for hd in range(RET_HEADS):
        lg = math.log(1.0 - 2.0 ** (-5.0 - hd))
        dmat = jnp.where(same, jnp.exp(lg * jnp.abs(dist)), jnp.where(earlier, jnp.exp(lg * dist), 0.0))
        qdec = jnp.exp(lg * (pos + 1.0)) * (RET_DK ** -0.5)
        kdec = jnp.exp(lg * (tb - 1.0 - pos))
        gblk = math.exp(lg * tb)
        cols = slice(hd * RET_DK, (hd + 1) * RET_DK)
        qh = q_all[:, cols].reshape(nb, tb, RET_DK)
        kh = k_all[:, cols].reshape(nb, tb, RET_DK)
        vh = v_all[:, cols].reshape(nb, tb, RET_DV)
        qh = qh * cos2 + pltpu.roll(qh, RET_DK // 2, axis=2) * sin2
        kh = kh * cos2 + pltpu.roll(kh, RET_DK // 2, axis=2) * sin2
        s_prev = ret_ref[:, hd]
        sc = _bdot((qh * (RET_DK ** -0.5)).astype(BF16), kh.astype(BF16), 2, 2) * dmat
        o = _bdot(sc.astype(BF16), vh, 2, 1) + _bdot((qh * qdec).astype(BF16), s_prev.astype(BF16), 2, 1)
        ret_ref[:, hd] = gblk * s_prev + _bdot((kh * kdec).astype(BF16), vh, 1, 1)
        o = _rms(o, gret[:, cols])
        gh = g_all[:, cols].reshape(nb, tb, RET_DV)
        reto_ref[:, cols] = (gh * _sigmoid(gh) * o).reshape(m, RET_DV)
        next(scan, None)
        next(scan, None)
        next(scan, None)
    for _ in scan:
        pass

    tiles_per_out = (MXU // S5_GROUP) * S5_STATE // LANE
    ut = jnp.concatenate([ut_ref[lt] for lt in range(D_S5 // LANE)], axis=1)
    for n in range(D_S5 // MXU):
        xr = jnp.concatenate([xre_ref[n * tiles_per_out + i] for i in range(tiles_per_out)], axis=1)
        xi = jnp.concatenate([xim_ref[n * tiles_per_out + i] for i in range(tiles_per_out)], axis=1)
        yn = _dot(xr.astype(BF16), wcre_ref[n]) + _dot(xi.astype(BF16), wcim_ref[n])
        yn = yn + dskip_ref[:, n * MXU:(n + 1) * MXU] * ut[:, n * MXU:(n + 1) * MXU]
        yt_ref[2 * n] = yn[:, :LANE]
        yt_ref[2 * n + 1] = yn[:, LANE:]
    y = jnp.concatenate(
        [jnp.concatenate([yt_ref.at[lt][pl.ds(b, tb, stride=nb), :] for lt in range(D_S5 // LANE)], axis=1)
         for b in range(nb)], axis=0)
    z = _gelu_tanh(y)
    z = z * _sigmoid(_dot(z.astype(BF16), wglu_ref[...]) + bglu_ref[...])
    s5_out = _rms(z, gs5_ref[...])

    mix = _dot(s5_out.astype(BF16), wout_ref[0:D_S5, :]) + _dot(reto_ref[...].astype(BF16), wout_ref[D_S5:, :])
    y_ref[...] = (x + _rms(mix, gpost_ref[...])).reshape(nb, tb, d)


def _s5_discretize(lam_re, lam_im, log_step, b_re, b_im, c_re, c_im):
    dt = jnp.exp(log_step)[:, None]
    ar = lam_re * dt
    ai = lam_im * dt
    mag = jnp.exp(ar)
    abar_re = mag * jnp.cos(ai)
    abar_im = mag * jnp.sin(ai)
    den = lam_re * lam_re + lam_im * lam_im
    nr = abar_re - 1.0
    ni = abar_im
    f_re = (nr * lam_re + ni * lam_im) / den
    f_im = (ni * lam_re - nr * lam_im) / den
    bb_re = f_re[..., None] * b_re - f_im[..., None] * b_im
    bb_im = f_re[..., None] * b_im + f_im[..., None] * b_re

    gpt = MXU // S5_STATE
    gps = LANE // S5_GROUP
    n_tiles = S5_LANES // MXU

    def pack_b(bb):
        bt = bb.transpose(0, 2, 1).reshape(n_tiles, gpt, S5_GROUP, S5_STATE)
        n_idx = jnp.arange(n_tiles)[:, None, None]
        gl = jnp.arange(gps)[None, :, None]
        gi = jnp.arange(gpt)[None, None, :]
        sel = (gl == (n_idx % (gps // gpt)) * gpt + gi).astype(F32)
        w = jnp.einsum('nlg,nghp->nlhgp', sel, bt)
        return w.reshape(n_tiles, LANE, MXU).astype(BF16)

    gpo = MXU // S5_GROUP
    n_out = D_S5 // MXU

    def pack_c(c):
        ct = c.reshape(n_out, gpo, S5_GROUP, S5_STATE)
        eye = jnp.eye(gpo, dtype=F32)
        w = jnp.einsum('lg,nghp->nlpgh', eye, ct)
        return w.reshape(n_out, gpo * S5_STATE, MXU).astype(BF16)

    def scan_rows(lo_half, hi_half):
        lo = jnp.broadcast_to(lo_half.reshape(N_LANE_TILES, 1, LANE), (N_LANE_TILES, SCAN_SEQS, LANE))
        hi = jnp.broadcast_to(hi_half.reshape(N_LANE_TILES, 1, LANE), (N_LANE_TILES, SCAN_SEQS, LANE))
        return jnp.concatenate([lo, hi], axis=1)

    wb = jnp.concatenate([pack_b(bb_re), pack_b(bb_im)], axis=1)
    return (scan_rows(abar_re, abar_re), scan_rows(-abar_im, abar_im), wb, pack_c(c_re), pack_c(-c_im))


def _rope_tables(pos0, length):
    half = RET_DK // 2
    inv = ROPE_BASE ** (-jnp.arange(half, dtype=F32) / half)
    ang = (pos0 + jnp.arange(length)).astype(F32)[:, None] * inv[None, :]
    cos = jnp.cos(ang)
    sin = jnp.sin(ang)
    return jnp.concatenate([cos, cos], axis=1), jnp.concatenate([-sin, sin], axis=1)


def _pack_s5_state(re, im):
    def tiles(a):
        return a.reshape(-1, SCAN_SEQS, N_LANE_TILES, LANE).transpose(0, 2, 1, 3)
    return jnp.concatenate([tiles(re), tiles(im)], axis=2)


def _unpack_s5_state(tiles):
    def seqs(a):
        return a.transpose(0, 2, 1, 3).reshape(-1, S5_GROUPS, S5_STATE)
    return seqs(tiles[:, :, :SCAN_SEQS]), seqs(tiles[:, :, SCAN_SEQS:])


def _mixer(x, tables, s5x0, ret0, w, layer, *, tb):
    nb, length, d = x.shape
    chunk = CHUNK if length >= CHUNK else length
    assert length % tb == 0 and tb % chunk == 0 and tb % 2 == 0 and nb % SCAN_SEQS == 0
    m = SCAN_SEQS * tb
    cos2, sin2 = tables
    consts = [w[k] for k in ('mix_norm_pre', 'mix_norm_post', 'w_in', 'a_rows', 'ais_rows', 'wb', 'wc_re', 'wc_im',
                             's5_d', 's5_w_glu', 's5_b_glu', 's5_out_norm', 'ret_out_norm', 'w_out')]
    x_spec = pl.BlockSpec((SCAN_SEQS, tb, d), lambda g, j: (g, j, 0))
    s5_in = pl.BlockSpec((None, 1) + s5x0.shape[2:], lambda g, j: (layer, g, 0, 0, 0))
    ret_in = pl.BlockSpec((None, SCAN_SEQS) + ret0.shape[2:], lambda g, j: (layer, g, 0, 0, 0))
    s5_out = pl.BlockSpec((1,) + s5x0.shape[2:], lambda g, j: (g, 0, 0, 0))
    ret_out = pl.BlockSpec((SCAN_SEQS,) + ret0.shape[2:], lambda g, j: (g, 0, 0, 0))
    in_specs = ([x_spec,
                 pl.BlockSpec((tb, RET_DK), lambda g, j: (j, 0)),
                 pl.BlockSpec((tb, RET_DK), lambda g, j: (j, 0)),
                 s5_in, ret_in]
                + [_layer_spec(c, layer) for c in consts])
    return pl.pallas_call(
        functools.partial(_mixer_kernel, chunk=chunk),
        grid=(nb // SCAN_SEQS, length // tb),
        in_specs=in_specs,
        out_specs=[x_spec, s5_out, ret_out],
        out_shape=[jax.ShapeDtypeStruct(x.shape, F32), jax.ShapeDtypeStruct(s5x0.shape[1:], F32),
                   jax.ShapeDtypeStruct(ret0.shape[1:], F32)],
        scratch_shapes=[pltpu.VMEM((D_S5 // LANE, m, LANE), F32),
                        pltpu.VMEM((N_LANE_TILES, tb // 2, 2, 2 * SCAN_SEQS, LANE), F32),
                        pltpu.VMEM((N_LANE_TILES, m, LANE), F32),
                        pltpu.VMEM((N_LANE_TILES, m, LANE), F32),
                        pltpu.VMEM((D_S5 // LANE, m, LANE), F32),
                        pltpu.VMEM((m, D_RET), F32)],
        compiler_params=pltpu.CompilerParams(dimension_semantics=("arbitrary", "arbitrary"),
                                             vmem_limit_bytes=VMEM_LIMIT),
        name="mixer",
    )(x, cos2, sin2, s5x0, ret0, *consts)


def _cross_kernel(x_ref, k_ref, v_ref, gpre_ref, gpost_ref, wq_ref, wo_ref, y_ref, o_ref, *, split_heads):
    _, tq, d = x_ref.shape
    dh = d // X_HEADS
    x = x_ref[0]
    h = _rms(x, gpre_ref[...]).astype(BF16)
    q = _dot(h, wq_ref[...]) * (dh ** -0.5)
    for hd in range(X_HEADS):
        cols = slice(hd * dh, (hd + 1) * dh)
        if split_heads:
            kh = k_ref[0, :, hd, :].astype(BF16)
            vh = v_ref[0, :, hd, :].astype(BF16)
        else:
            kh = k_ref[0, :, cols].astype(BF16)
            vh = v_ref[0, :, cols].astype(BF16)
        s = _dot_nt(q[:, cols].astype(BF16), kh)
        p = jnp.exp(s - jnp.max(s, axis=-1, keepdims=True))
        l = jnp.sum(p, axis=-1, keepdims=True)
        o_ref[:, cols] = _dot(p.astype(BF16), vh) / l
    out = _dot(o_ref[...].astype(BF16), wo_ref[...])
    y_ref[0] = x + _rms(out, gpost_ref[...])


def _cross(x, mem_k, mem_v, w, layer, *, tq):
    nb, length, d = x.shape
    assert length % tq == 0
    split_heads = mem_k.ndim == 5
    kv_tail = (0,) * (mem_k.ndim - 2)
    kv_spec = pl.BlockSpec((None, 1) + mem_k.shape[2:], lambda b, j: (layer, b) + kv_tail)
    consts = [w[k] for k in ('xattn_norm_pre', 'xattn_norm_post', 'w_cq', 'w_co')]
    return pl.pallas_call(
        functools.partial(_cross_kernel, split_heads=split_heads),
        grid=(nb, length // tq),
        in_specs=[pl.BlockSpec((1, tq, d), lambda b, j: (b, j, 0)), kv_spec, kv_spec]
                 + [_layer_spec(c, layer) for c in consts],
        out_specs=pl.BlockSpec((1, tq, d), lambda b, j: (b, j, 0)),
        out_shape=jax.ShapeDtypeStruct(x.shape, F32),
        scratch_shapes=[pltpu.VMEM((tq, d), F32)],
        compiler_params=pltpu.CompilerParams(dimension_semantics=("arbitrary", "arbitrary"),
                                             vmem_limit_bytes=VMEM_LIMIT),
        name="cross",
    )(x, mem_k, mem_v, *consts)


def _ffn_kernel(x_ref, gpre_ref, gpost_ref, wg_ref, wu_ref, wd_ref, y_ref):
    x = x_ref[...]
    h = _rms(x, gpre_ref[...]).astype(BF16)
    g = _dot(h, wg_ref[...])
    a = (g * _sigmoid(g) * _dot(h, wu_ref[...])).astype(BF16)
    f = _dot(a, wd_ref[...])
    y_ref[...] = x + _rms(f, gpost_ref[...])


def _ffn(x, w, layer, *, tm):
    shape = x.shape
    d = shape[-1]
    x2 = x.reshape(-1, d)
    rows = x2.shape[0]
    assert rows % tm == 0
    consts = [w[k] for k in ('ffn_norm_pre', 'ffn_norm_post', 'w_gate', 'w_up', 'w_down')]
    y = pl.pallas_call(
        _ffn_kernel,
        grid=(rows // tm,),
        in_specs=[pl.BlockSpec((tm, d), lambda j: (j, 0))] + [_layer_spec(c, layer) for c in consts],
        out_specs=pl.BlockSpec((tm, d), lambda j: (j, 0)),
        out_shape=jax.ShapeDtypeStruct(x2.shape, F32),
        compiler_params=pltpu.CompilerParams(dimension_semantics=("arbitrary",),
                                             vmem_limit_bytes=VMEM_LIMIT),
        name="ffn",
    )(x2, *consts)
    return y.reshape(shape)


def _memkv_kernel(mem_ref, g_ref, wk_ref, wv_ref, k_ref, v_ref, kb_ref, vb_ref):
    dh = k_ref.shape[-1]
    m = _rms(mem_ref[0], g_ref[0]).astype(BF16)
    for w_ref, out_ref, outb_ref in ((wk_ref, k_ref, kb_ref), (wv_ref, v_ref, vb_ref)):
        r = _dot(m, w_ref[0])
        outb_ref[0, 0] = r.astype(BF16)
        for hd in range(X_HEADS):
            out_ref[0, 0, :, hd, :] = r[:, hd * dh:(hd + 1) * dh]


def _memory_kv(mem, g_mem, w_ck, w_cv):
    depth = g_mem.shape[0]
    nb, n_mem, d = mem.shape
    dh = d // X_HEADS
    out = jax.ShapeDtypeStruct((depth, nb, n_mem, X_HEADS, dh), F32)
    outb = jax.ShapeDtypeStruct((depth, nb, n_mem, d), BF16)
    w_spec = pl.BlockSpec((1, d, d), lambda i, b: (i, 0, 0))
    return pl.pallas_call(
        _memkv_kernel,
        grid=(depth, nb),
        in_specs=[pl.BlockSpec((1, n_mem, d), lambda i, b: (b, 0, 0)),
                  pl.BlockSpec((1, 1, d), lambda i, b: (i, 0, 0)), w_spec, w_spec],
        out_specs=[pl.BlockSpec((1, 1, n_mem, X_HEADS, dh), lambda i, b: (i, b, 0, 0, 0))] * 2
                  + [pl.BlockSpec((1, 1, n_mem, d), lambda i, b: (i, b, 0, 0))] * 2,
        out_shape=[out, out, outb, outb],
        compiler_params=pltpu.CompilerParams(dimension_semantics=("arbitrary", "arbitrary"),
                                             vmem_limit_bytes=VMEM_LIMIT),
        name="memory_kv",
    )(mem, g_mem, w_ck, w_cv)


def _prepare_weights(p):
    depth = p['w_in'].shape[0]
    w = {}
    for k in ('w_in', 's5_w_glu', 'w_out', 'w_cq', 'w_co', 'w_gate', 'w_up', 'w_down'):
        w[k] = p[k].astype(BF16)
    for k in ('mix_norm_pre', 'mix_norm_post', 's5_d', 's5_b_glu', 's5_out_norm', 'ret_out_norm',
              'xattn_norm_pre', 'xattn_norm_post', 'ffn_norm_pre', 'ffn_norm_post'):
        w[k] = p[k].reshape(depth, 1, -1).astype(F32)
    (w['a_rows'], w['ais_rows'], w['wb'], w['wc_re'], w['wc_im']) = jax.vmap(_s5_discretize)(
        p['s5_lambda_re'], p['s5_lambda_im'], p['s5_log_step'],
        p['s5_b_re'], p['s5_b_im'], p['s5_c_re'], p['s5_c_im'])
    return w


def _block_rows(length, target):
    return target if length >= target and length % target == 0 else length


def _trunk(x, pos0, mem_k, mem_v, s5_re0, s5_im0, ret0, w):
    nb, length, d = x.shape
    depth = ret0.shape[0]
    tables = _rope_tables(pos0, length)
    tb = _block_rows(length, 128)
    tq = _block_rows(length, 512)
    tm = _block_rows(nb * length, 512)
    s5x0 = jax.vmap(_pack_s5_state)(s5_re0, s5_im0)
    s5r_all, s5i_all, ret_all = [], [], []
    for i in range(depth):
        x, s5x, s_ret = _mixer(x, tables, s5x0, ret0, w, i, tb=tb)
        s5r, s5i = _unpack_s5_state(s5x)
        x = _cross(x, mem_k, mem_v, w, i, tq=tq)
        x = _ffn(x, w, i, tm=tm)
        s5r_all.append(s5r)
        s5i_all.append(s5i)
        ret_all.append(s_ret)
    return x, jnp.stack(s5r_all), jnp.stack(s5i_all), jnp.stack(ret_all)


def kernel(x_prompt, x_sample, mem_prompt, state_s5_re, state_s5_im, state_ret, cache_mem_k, cache_mem_v,
           mix_norm_pre, mix_norm_post, w_in, s5_lambda_re, s5_lambda_im, s5_log_step,
           s5_b_re, s5_b_im, s5_c_re, s5_c_im, s5_d, s5_w_glu, s5_b_glu, s5_out_norm, ret_out_norm,
           w_out, xattn_norm_pre, xattn_norm_post, mem_norm, w_cq, w_ck, w_cv, w_co,
           ffn_norm_pre, ffn_norm_post, w_gate, w_up, w_down):
    p = {
        'mix_norm_pre': mix_norm_pre, 'mix_norm_post': mix_norm_post, 'w_in': w_in,
        's5_lambda_re': s5_lambda_re, 's5_lambda_im': s5_lambda_im, 's5_log_step': s5_log_step,
        's5_b_re': s5_b_re, 's5_b_im': s5_b_im, 's5_c_re': s5_c_re, 's5_c_im': s5_c_im, 's5_d': s5_d,
        's5_w_glu': s5_w_glu, 's5_b_glu': s5_b_glu, 's5_out_norm': s5_out_norm,
        'ret_out_norm': ret_out_norm, 'w_out': w_out,
        'xattn_norm_pre': xattn_norm_pre, 'xattn_norm_post': xattn_norm_post,
        'w_cq': w_cq, 'w_co': w_co,
        'ffn_norm_pre': ffn_norm_pre, 'ffn_norm_post': ffn_norm_post,
        'w_gate': w_gate, 'w_up': w_up, 'w_down': w_down,
    }
    depth = w_in.shape[0]
    nbp = x_prompt.shape[0]
    d = x_prompt.shape[-1]
    w = _prepare_weights(p)

    mk, mv, mk_bf16, mv_bf16 = _memory_kv(mem_prompt, mem_norm.reshape(depth, 1, d),
                                          w_ck.astype(BF16), w_cv.astype(BF16))
    zeros_s5 = jnp.zeros((depth, nbp, S5_GROUPS, S5_STATE), F32)
    zeros_ret = jnp.zeros((depth, nbp, RET_HEADS, RET_DK, RET_DV), F32)
    y_prompt, s5r_p, s5i_p, ret_p = _trunk(x_prompt, 0, mk_bf16, mv_bf16, zeros_s5, zeros_s5, zeros_ret, w)
    y_sample, s5r_s, s5i_s, ret_s = _trunk(x_sample, PAST_LEN, cache_mem_k, cache_mem_v,
                                           state_s5_re, state_s5_im, state_ret, w)
    return (y_prompt, y_sample, s5r_p, s5i_p, ret_p, mk, mv, s5r_s, s5i_s, ret_s)
```

```python
import functools
import math

import jax
import jax.numpy as jnp
from jax import lax
from jax.experimental import pallas as pl
from jax.experimental.pallas import tpu as pltpu

F32 = jnp.float32
BF16 = jnp.bfloat16

PAST_LEN = 1024
CHUNK = 64
D_S5 = 512
S5_GROUP = 16
S5_GROUPS = D_S5 // S5_GROUP
S5_STATE = 64
S5_LANES = S5_GROUPS * S5_STATE
RET_HEADS = 4
RET_DK = 128
RET_DV = 128
D_RET = RET_HEADS * RET_DV
X_HEADS = 4
ROPE_BASE = 10000.0
EPS = 1e-6

LANE = 128
MXU = 256
N_LANE_TILES = S5_LANES // LANE
SCAN_TILES = 8
SCAN_PIECE = 8
SCAN_SEQS = 4
SUB_BLOCKS = 2
VMEM_LIMIT = 56 * 1024 * 1024


def _layer_spec(arr, layer):
    tail = (0,) * (arr.ndim - 1)
    return pl.BlockSpec((None,) + arr.shape[1:], lambda *_: (layer,) + tail, pipeline_mode=pl.Buffered(1))


def _sub_blocks(rows):
    n = SUB_BLOCKS if rows % (8 * SUB_BLOCKS) == 0 else 1
    return [slice(i * rows // n, (i + 1) * rows // n) for i in range(n)]


def _rms(x, g):
    ms = jnp.mean(x * x, axis=-1, keepdims=True)
    return x * lax.rsqrt(ms + EPS) * g


def _dot(a, b):
    return jnp.dot(a, b, preferred_element_type=F32)


def _dot_nt(a, b):
    return lax.dot_general(a, b, (((1,), (1,)), ((), ())), preferred_element_type=F32)


def _bdot(a, b, ca, cb):
    return lax.dot_general(a, b, (((ca,), (cb,)), ((0,), (0,))), preferred_element_type=F32)


def _gelu_tanh(x):
    c = math.sqrt(2.0 / math.pi)
    return 0.5 * x * (1.0 + jnp.tanh(c * (x + 0.044715 * (x * x * x))))


def _sigmoid(x):
    return 1.0 / (1.0 + jnp.exp(-x))


def _mixer_kernel(x_ref, cos_ref, sin_ref, s5x0_ref, ret0_ref,
                  gpre_ref, gpost_ref, win_ref, ar_ref, ais_ref, wb_ref,
                  wcre_ref, wcim_ref, dskip_ref, wglu_ref, bglu_ref, gs5_ref, gret_ref, wout_ref,
                  y_ref, s5x_ref, ret_ref,
                  ut_ref, bu_ref, xre_ref, xim_ref, yt_ref, reto_ref, *, chunk):
    nb, tb, d = x_ref.shape
    assert nb == SCAN_SEQS
    m = nb * tb
    pairs = tb // 2

    @pl.when(pl.program_id(1) == 0)
    def _():
        s5x_ref[...] = s5x0_ref[...]
        ret_ref[...] = ret0_ref[...]

    x = x_ref[...].reshape(m, d)
    h = _rms(x, gpre_ref[...]).astype(BF16)

    u = _dot(h, win_ref[:, 0:D_S5])
    for lt in range(D_S5 // LANE):
        for b in range(nb):
            ut_ref.at[lt][pl.ds(b, tb, stride=nb), :] = u[b * tb:(b + 1) * tb, lt * LANE:(lt + 1) * LANE]

    lo = lax.broadcasted_iota(jnp.int32, (1, 2 * nb, LANE), 1) < nb
    for ks in range(D_S5 // LANE):
        t3 = ut_ref[ks].reshape(pairs, 2 * nb, LANE)
        r3 = pltpu.roll(t3, nb, axis=1)
        even = jnp.concatenate([jnp.where(lo, t3, 0.0), jnp.where(lo, 0.0, r3)], axis=-1)
        odd = jnp.concatenate([jnp.where(lo, r3, 0.0), jnp.where(lo, 0.0, t3)], axis=-1)
        for parity, lhs3 in ((0, even), (1, odd)):
            lhs = lhs3.reshape(pairs * 2 * nb, 2 * LANE).astype(BF16)
            for n in range(2 * ks, 2 * ks + 2):
                r = _dot(lhs, wb_ref[n]).reshape(pairs, 2 * nb, MXU)
                bu_ref[2 * n, :, parity] = r[:, :, :LANE]
                bu_ref[2 * n + 1, :, parity] = r[:, :, LANE:]

    lo2 = lax.broadcasted_iota(jnp.int32, (2 * nb, LANE), 0) < nb

    def scan_pieces(pairs_per_piece):
        for c0 in range(0, N_LANE_TILES, SCAN_TILES):
            tiles = range(c0, c0 + SCAN_TILES)
            state = {}
            for i in tiles:
                x0 = s5x_ref[0, i]
                state[i] = (x0, pltpu.roll(x0, nb, axis=0))
            for k in range(pairs):
                for i in tiles:
                    xc, xs = state[i]
                    ar = ar_ref[i]
                    ais = ais_ref[i]
                    xe = ar * xc + (ais * xs + bu_ref[i, k, 0])
                    xes = pltpu.roll(xe, nb, axis=0)
                    xo = ar * xe + (ais * xes + bu_ref[i, k, 1])
                    xos = pltpu.roll(xo, nb, axis=0)
                    rows = slice(k * 2 * nb, (k + 1) * 2 * nb)
                    xre_ref[i, rows, :] = jnp.where(lo2, xe, xos)
                    xim_ref[i, rows, :] = jnp.where(lo2, xes, xo)
                    state[i] = (xo, xos)
                if (k + 1) % pairs_per_piece == 0:
                    yield
            for i in tiles:
                s5x_ref[0, i] = state[i][0]

    o1 = D_S5
    o2 = o1 + RET_HEADS * RET_DK
    o3 = o2 + RET_HEADS * RET_DK
    o4 = o3 + D_RET
    scan = scan_pieces(SCAN_PIECE)
    q_all = _dot(h, win_ref[:, o1:o2])
    next(scan, None)
    k_all = _dot(h, win_ref[:, o2:o3])
    next(scan, None)
    v_all = _dot(h, win_ref[:, o3:o4]).astype(BF16)
    next(scan, None)
    g_all = _dot(h, win_ref[:, o4:o4 + D_RET])
    next(scan, None)
    cos2 = cos_ref[...]
    sin2 = sin_ref[...]
    ri = lax.broadcasted_iota(jnp.int32, (tb, tb), 0)
    ci = lax.broadcasted_iota(jnp.int32, (tb, tb), 1)
    same = (ri // chunk) == (ci // chunk)
    earlier = (ci // chunk) < (ri // chunk)
    dist = (ri - ci).astype(F32)
    pos = lax.broadcasted_iota(jnp.int32, (tb, 1), 0).astype(F32)
    gret = gret_ref[...]
    for hd in range(RET_HEADS):
        lg = math.log(1.0 - 2.0 ** (-5.0 - hd))
        dmat = jnp.where(same, jnp.exp(lg * jnp.abs(dist)), jnp.where(earlier, jnp.exp(lg * dist), 0.0))
        qdec = jnp.exp(lg * (pos + 1.0)) * (RET_DK ** -0.5)
        kdec = jnp.exp(lg * (tb - 1.0 - pos))
        gblk = math.exp(lg * tb)
        cols = slice(hd * RET_DK, (hd + 1) * RET_DK)
        qh = q_all[:, cols].reshape(nb, tb, RET_DK)
        kh = k_all[:, cols].reshape(nb, tb, RET_DK)
        vh = v_all[:, cols].reshape(nb, tb, RET_DV)
        qh = qh * cos2 + pltpu.roll(qh, RET_DK // 2, axis=2) * sin2
        kh = kh * cos2 + pltpu.roll(kh, RET_DK // 2, axis=2) * sin2
        s_prev = ret_ref[:, hd]
        sc = _bdot((qh * (RET_DK ** -0.5)).astype(BF16), kh.astype(BF16), 2, 2) * dmat
        o = _bdot(sc.astype(BF16), vh, 2, 1) + _bdot((qh * qdec).astype(BF16), s_prev.astype(BF16), 2, 1)
        ret_ref[:, hd] = gblk * s_prev + _bdot((kh * kdec).astype(BF16), vh, 1, 1)
        o = _rms(o, gret[:, cols])
        gh = g_all[:, cols].reshape(nb, tb, RET_DV)
        reto_ref[:, cols] = (gh * _sigmoid(gh) * o).reshape(m, RET_DV)
        next(scan, None)
        next(scan, None)
        next(scan, None)
    for _ in scan:
        pass

    tiles_per_out = (MXU // S5_GROUP) * S5_STATE // LANE
    ut = jnp.concatenate([ut_ref[lt] for lt in range(D_S5 // LANE)], axis=1)
    for n in range(D_S5 // MXU):
        xr = jnp.concatenate([xre_ref[n * tiles_per_out + i] for i in range(tiles_per_out)], axis=1)
        xi = jnp.concatenate([xim_ref[n * tiles_per_out + i] for i in range(tiles_per_out)], axis=1)
        yn = _dot(xr.astype(BF16), wcre_ref[n]) + _dot(xi.astype(BF16), wcim_ref[n])
        yn = yn + dskip_ref[:, n * MXU:(n + 1) * MXU] * ut[:, n * MXU:(n + 1) * MXU]
        yt_ref[2 * n] = yn[:, :LANE]
        yt_ref[2 * n + 1] = yn[:, LANE:]
    y = jnp.concatenate(
        [jnp.concatenate([yt_ref.at[lt][pl.ds(b, tb, stride=nb), :] for lt in range(D_S5 // LANE)], axis=1)
         for b in range(nb)], axis=0)
    z = _gelu_tanh(y)
    z = z * _sigmoid(_dot(z.astype(BF16), wglu_ref[...]) + bglu_ref[...])
    s5_out = _rms(z, gs5_ref[...])

    mix = _dot(s5_out.astype(BF16), wout_ref[0:D_S5, :]) + _dot(reto_ref[...].astype(BF16), wout_ref[D_S5:, :])
    y_ref[...] = (x + _rms(mix, gpost_ref[...])).reshape(nb, tb, d)


def _s5_discretize(lam_re, lam_im, log_step, b_re, b_im, c_re, c_im):
    dt = jnp.exp(log_step)[:, None]
    ar = lam_re * dt
    ai = lam_im * dt
    mag = jnp.exp(ar)
    abar_re = mag * jnp.cos(ai)
    abar_im = mag * jnp.sin(ai)
    den = lam_re * lam_re + lam_im * lam_im
    nr = abar_re - 1.0
    ni = abar_im
    f_re = (nr * lam_re + ni * lam_im) / den
    f_im = (ni * lam_re - nr * lam_im) / den
    bb_re = f_re[..., None] * b_re - f_im[..., None] * b_im
    bb_im = f_re[..., None] * b_im + f_im[..., None] * b_re

    gpt = MXU // S5_STATE
    gps = LANE // S5_GROUP
    n_tiles = S5_LANES // MXU

    def pack_b(bb):
        bt = bb.transpose(0, 2, 1).reshape(n_tiles, gpt, S5_GROUP, S5_STATE)
        n_idx = jnp.arange(n_tiles)[:, None, None]
        gl = jnp.arange(gps)[None, :, None]
        gi = jnp.arange(gpt)[None, None, :]
        sel = (gl == (n_idx % (gps // gpt)) * gpt + gi).astype(F32)
        w = jnp.einsum('nlg,nghp->nlhgp', sel, bt)
        return w.reshape(n_tiles, LANE, MXU).astype(BF16)

    gpo = MXU // S5_GROUP
    n_out = D_S5 // MXU

    def pack_c(c):
        ct = c.reshape(n_out, gpo, S5_GROUP, S5_STATE)
        eye = jnp.eye(gpo, dtype=F32)
        w = jnp.einsum('lg,nghp->nlpgh', eye, ct)
        return w.reshape(n_out, gpo * S5_STATE, MXU).astype(BF16)

    def scan_rows(lo_half, hi_half):
        lo = jnp.broadcast_to(lo_half.reshape(N_LANE_TILES, 1, LANE), (N_LANE_TILES, SCAN_SEQS, LANE))
        hi = jnp.broadcast_to(hi_half.reshape(N_LANE_TILES, 1, LANE), (N_LANE_TILES, SCAN_SEQS, LANE))
        return jnp.concatenate([lo, hi], axis=1)

    wb = jnp.concatenate([pack_b(bb_re), pack_b(bb_im)], axis=1)
    return (scan_rows(abar_re, abar_re), scan_rows(-abar_im, abar_im), wb, pack_c(c_re), pack_c(-c_im))


def _rope_tables(pos0, length):
    half = RET_DK // 2
    inv = ROPE_BASE ** (-jnp.arange(half, dtype=F32) / half)
    ang = (pos0 + jnp.arange(length)).astype(F32)[:, None] * inv[None, :]
    cos = jnp.cos(ang)
    sin = jnp.sin(ang)
    return jnp.concatenate([cos, cos], axis=1), jnp.concatenate([-sin, sin], axis=1)


def _pack_s5_state(re, im):
    def tiles(a):
        return a.reshape(-1, SCAN_SEQS, N_LANE_TILES, LANE).transpose(0, 2, 1, 3)
    return jnp.concatenate([tiles(re), tiles(im)], axis=2)


def _unpack_s5_state(tiles):
    def seqs(a):
        return a.transpose(0, 2, 1, 3).reshape(-1, S5_GROUPS, S5_STATE)
    return seqs(tiles[:, :, :SCAN_SEQS]), seqs(tiles[:, :, SCAN_SEQS:])


def _mixer(x, tables, s5x0, ret0, w, layer, *, tb):
    nb, length, d = x.shape
    chunk = CHUNK if length >= CHUNK else length
    assert length % tb == 0 and tb % chunk == 0 and tb % 2 == 0 and nb % SCAN_SEQS == 0
    m = SCAN_SEQS * tb
    cos2, sin2 = tables
    consts = [w[k] for k in ('mix_norm_pre', 'mix_norm_post', 'w_in', 'a_rows', 'ais_rows', 'wb', 'wc_re', 'wc_im',
                             's5_d', 's5_w_glu', 's5_b_glu', 's5_out_norm', 'ret_out_norm', 'w_out')]
    x_spec = pl.BlockSpec((SCAN_SEQS, tb, d), lambda g, j: (g, j, 0))
    s5_in = pl.BlockSpec((None, 1) + s5x0.shape[2:], lambda g, j: (layer, g, 0, 0, 0))
    ret_in = pl.BlockSpec((None, SCAN_SEQS) + ret0.shape[2:], lambda g, j: (layer, g, 0, 0, 0))
    s5_out = pl.BlockSpec((1,) + s5x0.shape[2:], lambda g, j: (g, 0, 0, 0))
    ret_out = pl.BlockSpec((SCAN_SEQS,) + ret0.shape[2:], lambda g, j: (g, 0, 0, 0))
    in_specs = ([x_spec,
                 pl.BlockSpec((tb, RET_DK), lambda g, j: (j, 0)),
                 pl.BlockSpec((tb, RET_DK), lambda g, j: (j, 0)),
                 s5_in, ret_in]
                + [_layer_spec(c, layer) for c in consts])
    return pl.pallas_call(
        functools.partial(_mixer_kernel, chunk=chunk),
        grid=(nb // SCAN_SEQS, length // tb),
        in_specs=in_specs,
        out_specs=[x_spec, s5_out, ret_out],
        out_shape=[jax.ShapeDtypeStruct(x.shape, F32), jax.ShapeDtypeStruct(s5x0.shape[1:], F32),
                   jax.ShapeDtypeStruct(ret0.shape[1:], F32)],
        scratch_shapes=[pltpu.VMEM((D_S5 // LANE, m, LANE), F32),
                        pltpu.VMEM((N_LANE_TILES, tb // 2, 2, 2 * SCAN_SEQS, LANE), F32),
                        pltpu.VMEM((N_LANE_TILES, m, LANE), F32),
                        pltpu.VMEM((N_LANE_TILES, m, LANE), F32),
                        pltpu.VMEM((D_S5 // LANE, m, LANE), F32),
                        pltpu.VMEM((m, D_RET), F32)],
        compiler_params=pltpu.CompilerParams(dimension_semantics=("arbitrary", "arbitrary"),
                                             vmem_limit_bytes=VMEM_LIMIT),
        name="mixer",
    )(x, cos2, sin2, s5x0, ret0, *consts)


def _cross_kernel(x_ref, k_ref, v_ref, gpre_ref, gpost_ref, wq_ref, wo_ref, y_ref, o_ref):
    _, tq, d = x_ref.shape
    dh = d // X_HEADS
    subs = _sub_blocks(tq)
    xs = [x_ref[0, rows, :] for rows in subs]
    qs = [_dot(_rms(x, gpre_ref[...]).astype(BF16), wq_ref[...]) * (dh ** -0.5) for x in xs]
    for hd in range(X_HEADS):
        cols = slice(hd * dh, (hd + 1) * dh)
        for rows, q in zip(subs, qs):
            s = _dot_nt(q[:, cols].astype(BF16), k_ref[0, :, cols])
            p = jnp.exp(s - jnp.max(s, axis=-1, keepdims=True))
            l = jnp.sum(p, axis=-1, keepdims=True)
            o_ref[rows, cols] = _dot(p.astype(BF16), v_ref[0, :, cols]) / l
    outs = [_dot(o_ref[rows, :].astype(BF16), wo_ref[...]) for rows in subs]
    for rows, x, out in zip(subs, xs, outs):
        y_ref[0, rows, :] = x + _rms(out, gpost_ref[...])


def _cross(x, mem_k, mem_v, w, layer, *, tq):
    nb, length, d = x.shape
    assert length % tq == 0
    kv_spec = pl.BlockSpec((None, 1) + mem_k.shape[2:], lambda b, j: (layer, b, 0, 0))
    consts = [w[k] for k in ('xattn_norm_pre', 'xattn_norm_post', 'w_cq', 'w_co')]
    return pl.pallas_call(
        _cross_kernel,
        grid=(nb, length // tq),
        in_specs=[pl.BlockSpec((1, tq, d), lambda b, j: (b, j, 0)), kv_spec, kv_spec]
                 + [_layer_spec(c, layer) for c in consts],
        out_specs=pl.BlockSpec((1, tq, d), lambda b, j: (b, j, 0)),
        out_shape=jax.ShapeDtypeStruct(x.shape, F32),
        scratch_shapes=[pltpu.VMEM((tq, d), F32)],
        compiler_params=pltpu.CompilerParams(dimension_semantics=("arbitrary", "arbitrary"),
                                             vmem_limit_bytes=VMEM_LIMIT),
        name="cross",
    )(x, mem_k, mem_v, *consts)


def _ffn_kernel(x_ref, gpre_ref, gpost_ref, wg_ref, wu_ref, wd_ref, y_ref):
    subs = _sub_blocks(x_ref.shape[0])
    xs = [x_ref[rows, :] for rows in subs]
    hs = [_rms(x, gpre_ref[...]).astype(BF16) for x in xs]
    gs = [_dot(h, wg_ref[...]) for h in hs]
    acts = [(g * _sigmoid(g) * _dot(h, wu_ref[...])).astype(BF16) for g, h in zip(gs, hs)]
    fs = [_dot(a, wd_ref[...]) for a in acts]
    for rows, x, f in zip(subs, xs, fs):
        y_ref[rows, :] = x + _rms(f, gpost_ref[...])


def _ffn(x, w, layer, *, tm):
    shape = x.shape
    d = shape[-1]
    x2 = x.reshape(-1, d)
    rows = x2.shape[0]
    assert rows % tm == 0
    consts = [w[k] for k in ('ffn_norm_pre', 'ffn_norm_post', 'w_gate', 'w_up', 'w_down')]
    y = pl.pallas_call(
        _ffn_kernel,
        grid=(rows // tm,),
        in_specs=[pl.BlockSpec((tm, d), lambda j: (j, 0))] + [_layer_spec(c, layer) for c in consts],
        out_specs=pl.BlockSpec((tm, d), lambda j: (j, 0)),
        out_shape=jax.ShapeDtypeStruct(x2.shape, F32),
        compiler_params=pltpu.CompilerParams(dimension_semantics=("arbitrary",),
                                             vmem_limit_bytes=VMEM_LIMIT),
        name="ffn",
    )(x2, *consts)
    return y.reshape(shape)


def _memkv_kernel(mem_ref, g_ref, wk_ref, wv_ref, k_ref, v_ref, kb_ref, vb_ref):
    dh = k_ref.shape[-1]
    m = _rms(mem_ref[0], g_ref[0]).astype(BF16)
    for w_ref, out_ref, outb_ref in ((wk_ref, k_ref, kb_ref), (wv_ref, v_ref, vb_ref)):
        r = _dot(m, w_ref[0])
        outb_ref[0, 0] = r.astype(BF16)
        for hd in range(X_HEADS):
            out_ref[0, 0, :, hd, :] = r[:, hd * dh:(hd + 1) * dh]


def _memory_kv(mem, g_mem, w_ck, w_cv):
    depth = g_mem.shape[0]
    nb, n_mem, d = mem.shape
    dh = d // X_HEADS
    out = jax.ShapeDtypeStruct((depth, nb, n_mem, X_HEADS, dh), F32)
    outb = jax.ShapeDtypeStruct((depth, nb, n_mem, d), BF16)
    w_spec = pl.BlockSpec((1, d, d), lambda i, b: (i, 0, 0))
    return pl.pallas_call(
        _memkv_kernel,
        grid=(depth, nb),
        in_specs=[pl.BlockSpec((1, n_mem, d), lambda i, b: (b, 0, 0)),
                  pl.BlockSpec((1, 1, d), lambda i, b: (i, 0, 0)), w_spec, w_spec],
        out_specs=[pl.BlockSpec((1, 1, n_mem, X_HEADS, dh), lambda i, b: (i, b, 0, 0, 0))] * 2
                  + [pl.BlockSpec((1, 1, n_mem, d), lambda i, b: (i, b, 0, 0))] * 2,
        out_shape=[out, out, outb, outb],
        compiler_params=pltpu.CompilerParams(dimension_semantics=("arbitrary", "arbitrary"),
                                             vmem_limit_bytes=VMEM_LIMIT),
        name="memory_kv",
    )(mem, g_mem, w_ck, w_cv)


def _prepare_weights(p):
    depth = p['w_in'].shape[0]
    w = {}
    for k in ('w_in', 's5_w_glu', 'w_out', 'w_cq', 'w_co', 'w_gate', 'w_up', 'w_down'):
        w[k] = p[k].astype(BF16)
    for k in ('mix_norm_pre', 'mix_norm_post', 's5_d', 's5_b_glu', 's5_out_norm', 'ret_out_norm',
              'xattn_norm_pre', 'xattn_norm_post', 'ffn_norm_pre', 'ffn_norm_post'):
        w[k] = p[k].reshape(depth, 1, -1).astype(F32)
    (w['a_rows'], w['ais_rows'], w['wb'], w['wc_re'], w['wc_im']) = jax.vmap(_s5_discretize)(
        p['s5_lambda_re'], p['s5_lambda_im'], p['s5_log_step'],
        p['s5_b_re'], p['s5_b_im'], p['s5_c_re'], p['s5_c_im'])
    return w


def _block_rows(length, target):
    return target if length >= target and length % target == 0 else length


def _trunk(x, pos0, mem_k, mem_v, s5_re0, s5_im0, ret0, w):
    nb, length, d = x.shape
    depth = ret0.shape[0]
    tables = _rope_tables(pos0, length)
    tb = _block_rows(length, 128)
    tq = _block_rows(length, 512)
    tm = _block_rows(nb * length, 512)
    s5x0 = jax.vmap(_pack_s5_state)(s5_re0, s5_im0)
    s5r_all, s5i_all, ret_all = [], [], []
    for i in range(depth):
        x, s5x, s_ret = _mixer(x, tables, s5x0, ret0, w, i, tb=tb)
        s5r, s5i = _unpack_s5_state(s5x)
        x = _cross(x, mem_k, mem_v, w, i, tq=tq)
        x = _ffn(x, w, i, tm=tm)
        s5r_all.append(s5r)
        s5i_all.append(s5i)
        ret_all.append(s_ret)
    return x, jnp.stack(s5r_all), jnp.stack(s5i_all), jnp.stack(ret_all)


def kernel(x_prompt, x_sample, mem_prompt, state_s5_re, state_s5_im, state_ret, cache_mem_k, cache_mem_v,
           mix_norm_pre, mix_norm_post, w_in, s5_lambda_re, s5_lambda_im, s5_log_step,
           s5_b_re, s5_b_im, s5_c_re, s5_c_im, s5_d, s5_w_glu, s5_b_glu, s5_out_norm, ret_out_norm,
           w_out, xattn_norm_pre, xattn_norm_post, mem_norm, w_cq, w_ck, w_cv, w_co,
           ffn_norm_pre, ffn_norm_post, w_gate, w_up, w_down):
    p = {
        'mix_norm_pre': mix_norm_pre, 'mix_norm_post': mix_norm_post, 'w_in': w_in,
        's5_lambda_re': s5_lambda_re, 's5_lambda_im': s5_lambda_im, 's5_log_step': s5_log_step,
        's5_b_re': s5_b_re, 's5_b_im': s5_b_im, 's5_c_re': s5_c_re, 's5_c_im': s5_c_im, 's5_d': s5_d,
        's5_w_glu': s5_w_glu, 's5_b_glu': s5_b_glu, 's5_out_norm': s5_out_norm,
        'ret_out_norm': ret_out_norm, 'w_out': w_out,
        'xattn_norm_pre': xattn_norm_pre, 'xattn_norm_post': xattn_norm_post,
        'w_cq': w_cq, 'w_co': w_co,
        'ffn_norm_pre': ffn_norm_pre, 'ffn_norm_post': ffn_norm_post,
        'w_gate': w_gate, 'w_up': w_up, 'w_down': w_down,
    }
    depth = w_in.shape[0]
    nbp = x_prompt.shape[0]
    d = x_prompt.shape[-1]
    w = _prepare_weights(p)

    mk, mv, mk_bf16, mv_bf16 = _memory_kv(mem_prompt, mem_norm.reshape(depth, 1, d),
                                          w_ck.astype(BF16), w_cv.astype(BF16))
    zeros_s5 = jnp.zeros((depth, nbp, S5_GROUPS, S5_STATE), F32)
    zeros_ret = jnp.zeros((depth, nbp, RET_HEADS, RET_DK, RET_DV), F32)
    y_prompt, s5r_p, s5i_p, ret_p = _trunk(x_prompt, 0, mk_bf16, mv_bf16, zeros_s5, zeros_s5, zeros_ret, w)
    kv_flat = cache_mem_k.shape[:3] + (d,)
    y_sample, s5r_s, s5i_s, ret_s = _trunk(x_sample, PAST_LEN, cache_mem_k.astype(BF16).reshape(kv_flat),
                                           cache_mem_v.astype(BF16).reshape(kv_flat),
                                           state_s5_re, state_s5_im, state_ret, w)
    return (y_prompt, y_sample, s5r_p, s5i_p, ret_p, mk, mv, s5r_s, s5i_s, ret_s)
```

```python
import functools
import math

import jax
import jax.numpy as jnp
from jax import lax
from jax.experimental import pallas as pl
from jax.experimental.pallas import tpu as pltpu

F32 = jnp.float32
BF16 = jnp.bfloat16

PAST_LEN = 1024
CHUNK = 64
D_S5 = 512
S5_GROUP = 16
S5_GROUPS = D_S5 // S5_GROUP
S5_STATE = 64
S5_LANES = S5_GROUPS * S5_STATE
RET_HEADS = 4
RET_DK = 128
RET_DV = 128
D_RET = RET_HEADS * RET_DV
X_HEADS = 4
ROPE_BASE = 10000.0
EPS = 1e-6

LANE = 128
MXU = 256
N_LANE_TILES = S5_LANES // LANE
SCAN_TILES = 8
SCAN_PIECE = 8
SCAN_SEQS = 4
SUB_BLOCKS = 2
VMEM_LIMIT = 56 * 1024 * 1024


def _layer_spec(arr, layer):
    tail = (0,) * (arr.ndim - 1)
    return pl.BlockSpec((None,) + arr.shape[1:], lambda *_: (layer,) + tail, pipeline_mode=pl.Buffered(1))


def _sub_blocks(rows):
    n = SUB_BLOCKS if rows % (8 * SUB_BLOCKS) == 0 else 1
    return [slice(i * rows // n, (i + 1) * rows // n) for i in range(n)]


def _rms(x, g):
    ms = jnp.mean(x * x, axis=-1, keepdims=True)
    return x * lax.rsqrt(ms + EPS) * g


def _dot(a, b):
    return jnp.dot(a, b, preferred_element_type=F32)


def _dot_nt(a, b):
    return lax.dot_general(a, b, (((1,), (1,)), ((), ())), preferred_element_type=F32)


def _bdot(a, b, ca, cb):
    return lax.dot_general(a, b, (((ca,), (cb,)), ((0,), (0,))), preferred_element_type=F32)


def _gelu_tanh(x):
    c = math.sqrt(2.0 / math.pi)
    return 0.5 * x * (1.0 + jnp.tanh(c * (x + 0.044715 * (x * x * x))))


def _sigmoid(x):
    return 1.0 / (1.0 + jnp.exp(-x))


def _mixer_kernel(x_ref, cos_ref, sin_ref, s5x0_ref, ret0_ref,
                  gpre_ref, gpost_ref, win_ref, ar_ref, ais_ref, wb_ref,
                  wcre_ref, wcim_ref, dskip_ref, wglu_ref, bglu_ref, gs5_ref, gret_ref, wout_ref,
                  y_ref, s5x_ref, ret_ref,
                  ut_ref, bu_ref, xre_ref, xim_ref, yt_ref, reto_ref, *, chunk):
    nb, tb, d = x_ref.shape
    assert nb == SCAN_SEQS
    m = nb * tb
    pairs = tb // 2

    @pl.when(pl.program_id(1) == 0)
    def _():
        s5x_ref[...] = s5x0_ref[...]
        ret_ref[...] = ret0_ref[...]

    x = x_ref[...].reshape(m, d)
    h = _rms(x, gpre_ref[...]).astype(BF16)

    u = _dot(h, win_ref[:, 0:D_S5])
    for lt in range(D_S5 // LANE):
        for b in range(nb):
            ut_ref.at[lt][pl.ds(b, tb, stride=nb), :] = u[b * tb:(b + 1) * tb, lt * LANE:(lt + 1) * LANE]

    lo = lax.broadcasted_iota(jnp.int32, (1, 2 * nb, LANE), 1) < nb
    for ks in range(D_S5 // LANE):
        t3 = ut_ref[ks].reshape(pairs, 2 * nb, LANE)
        r3 = pltpu.roll(t3, nb, axis=1)
        even = jnp.concatenate([jnp.where(lo, t3, 0.0), jnp.where(lo, 0.0, r3)], axis=-1)
        odd = jnp.concatenate([jnp.where(lo, r3, 0.0), jnp.where(lo, 0.0, t3)], axis=-1)
        for parity, lhs3 in ((0, even), (1, odd)):
            lhs = lhs3.reshape(pairs * 2 * nb, 2 * LANE).astype(BF16)
            for n in range(2 * ks, 2 * ks + 2):
                r = _dot(lhs, wb_ref[n]).reshape(pairs, 2 * nb, MXU)
                bu_ref[2 * n, :, parity] = r[:, :, :LANE]
                bu_ref[2 * n + 1, :, parity] = r[:, :, LANE:]

    lo2 = lax.broadcasted_iota(jnp.int32, (2 * nb, LANE), 0) < nb

    def scan_pieces(pairs_per_piece):
        for c0 in range(0, N_LANE_TILES, SCAN_TILES):
            tiles = range(c0, c0 + SCAN_TILES)
            state = {}
            for i in tiles:
                x0 = s5x_ref[0, i]
                state[i] = (x0, pltpu.roll(x0, nb, axis=0))
            for k in range(pairs):
                for i in tiles:
                    xc, xs = state[i]
                    ar = ar_ref[i]
                    ais = ais_ref[i]
                    xe = ar * xc + (ais * xs + bu_ref[i, k, 0])
                    xes = pltpu.roll(xe, nb, axis=0)
                    xo = ar * xe + (ais * xes + bu_ref[i, k, 1])
                    xos = pltpu.roll(xo, nb, axis=0)
                    rows = slice(k * 2 * nb, (k + 1) * 2 * nb)
                    xre_ref[i, rows, :] = jnp.where(lo2, xe, xos)
                    xim_ref[i, rows, :] = jnp.where(lo2, xes, xo)
                    state[i] = (xo, xos)
                if (k + 1) % pairs_per_piece == 0:
                    yield
            for i in tiles:
                s5x_ref[0, i] = state[i][0]

    o1 = D_S5
    o2 = o1 + RET_HEADS * RET_DK
    o3 = o2 + RET_HEADS * RET_DK
    o4 = o3 + D_RET
    scan = scan_pieces(SCAN_PIECE)
    q_all = _dot(h, win_ref[:, o1:o2])
    next(scan, None)
    k_all = _dot(h, win_ref[:, o2:o3])
    next(scan, None)
    v_all = _dot(h, win_ref[:, o3:o4]).astype(BF16)
    next(scan, None)
    g_all = _dot(h, win_ref[:, o4:o4 + D_RET])
    next(scan, None)
    cos2 = cos_ref[...]
    sin2 = sin_ref[...]
    ri = lax.broadcasted_iota(jnp.int32, (tb, tb), 0)
    ci = lax.broadcasted_iota(jnp.int32, (tb, tb), 1)
    same = (ri // chunk) == (ci // chunk)
    earlier = (ci // chunk) < (ri // chunk)
    dist = (ri - ci).astype(F32)
    pos = lax.broadcasted_iota(jnp.int32, (tb, 1), 0).astype(F32)
    gret = gret_ref[...]
    for hd in range(RET_HEADS):
        lg = math.log(1.0 - 2.0 ** (-5.0 - hd))
        dmat = jnp.where(same, jnp.exp(lg * jnp.abs(dist)), jnp.where(earlier, jnp.exp(lg * dist), 0.0))
        qdec = jnp.exp(lg * (pos + 1.0)) * (RET_DK ** -0.5)
        kdec = jnp.exp(lg * (tb - 1.0 - pos))
        gblk = math.exp(lg * tb)
        cols = slice(hd * RET_DK, (hd + 1) * RET_DK)
        qh = q_all[:, cols].reshape(nb, tb, RET_DK)
        kh = k_all[:, cols].reshape(nb, tb, RET_DK)
        vh = v_all[:, cols].reshape(nb, tb, RET_DV)
        qh = qh * cos2 + pltpu.roll(qh, RET_DK // 2, axis=2) * sin2
        kh = kh * cos2 + pltpu.roll(kh, RET_DK // 2, axis=2) * sin2
        s_prev = ret_ref[:, hd]
        sc = _bdot((qh * (RET_DK ** -0.5)).astype(BF16), kh.astype(BF16), 2, 2) * dmat
        o = _bdot(sc.astype(BF16), vh, 2, 1) + _bdot((qh * qdec).astype(BF16), s_prev.astype(BF16), 2, 1)
        ret_ref[:, hd] = gblk * s_prev + _bdot((kh * kdec).astype(BF16), vh, 1, 1)
        o = _rms(o, gret[:, cols])
        gh = g_all[:, cols].reshape(nb, tb, RET_DV)
        reto_ref[:, cols] = (gh * _sigmoid(gh) * o).reshape(m, RET_DV)
        next(scan, None)
        next(scan, None)
        next(scan, None)
    for _ in scan:
        pass

    tiles_per_out = (MXU // S5_GROUP) * S5_STATE // LANE
    ut = jnp.concatenate([ut_ref[lt] for lt in range(D_S5 // LANE)], axis=1)
    for n in range(D_S5 // MXU):
        xr = jnp.concatenate([xre_ref[n * tiles_per_out + i] for i in range(tiles_per_out)], axis=1)
        xi = jnp.concatenate([xim_ref[n * tiles_per_out + i] for i in range(tiles_per_out)], axis=1)
        yn = _dot(xr.astype(BF16), wcre_ref[n]) + _dot(xi.astype(BF16), wcim_ref[n])
        yn = yn + dskip_ref[:, n * MXU:(n + 1) * MXU] * ut[:, n * MXU:(n + 1) * MXU]
        yt_ref[2 * n] = yn[:, :LANE]
        yt_ref[2 * n + 1] = yn[:, LANE:]
    y = jnp.concatenate(
        [jnp.concatenate([yt_ref.at[lt][pl.ds(b, tb, stride=nb), :] for lt in range(D_S5 // LANE)], axis=1)
         for b in range(nb)], axis=0)
    z = _gelu_tanh(y)
    z = z * _sigmoid(_dot(z.astype(BF16), wglu_ref[...]) + bglu_ref[...])
    s5_out = _rms(z, gs5_ref[...])

    mix = _dot(s5_out.astype(BF16), wout_ref[0:D_S5, :]) + _dot(reto_ref[...].astype(BF16), wout_ref[D_S5:, :])
    y_ref[...] = (x + _rms(mix, gpost_ref[...])).reshape(nb, tb, d)


def _s5_discretize(lam_re, lam_im, log_step, b_re, b_im, c_re, c_im):
    dt = jnp.exp(log_step)[:, None]
    ar = lam_re * dt
    ai = lam_im * dt
    mag = jnp.exp(ar)
    abar_re = mag * jnp.cos(ai)
    abar_im = mag * jnp.sin(ai)
    den = lam_re * lam_re + lam_im * lam_im
    nr = abar_re - 1.0
    ni = abar_im
    f_re = (nr * lam_re + ni * lam_im) / den
    f_im = (ni * lam_re - nr * lam_im) / den
    bb_re = f_re[..., None] * b_re - f_im[..., None] * b_im
    bb_im = f_re[..., None] * b_im + f_im[..., None] * b_re

    gpt = MXU // S5_STATE
    gps = LANE // S5_GROUP
    n_tiles = S5_LANES // MXU

    def pack_b(bb):
        bt = bb.transpose(0, 2, 1).reshape(n_tiles, gpt, S5_GROUP, S5_STATE)
        n_idx = jnp.arange(n_tiles)[:, None, None]
        gl = jnp.arange(gps)[None, :, None]
        gi = jnp.arange(gpt)[None, None, :]
        sel = (gl == (n_idx % (gps // gpt)) * gpt + gi).astype(F32)
        w = jnp.einsum('nlg,nghp->nlhgp', sel, bt)
        return w.reshape(n_tiles, LANE, MXU).astype(BF16)

    gpo = MXU // S5_GROUP
    n_out = D_S5 // MXU

    def pack_c(c):
        ct = c.reshape(n_out, gpo, S5_GROUP, S5_STATE)
        eye = jnp.eye(gpo, dtype=F32)
        w = jnp.einsum('lg,nghp->nlpgh', eye, ct)
        return w.reshape(n_out, gpo * S5_STATE, MXU).astype(BF16)

    def scan_rows(lo_half, hi_half):
        lo = jnp.broadcast_to(lo_half.reshape(N_LANE_TILES, 1, LANE), (N_LANE_TILES, SCAN_SEQS, LANE))
        hi = jnp.broadcast_to(hi_half.reshape(N_LANE_TILES, 1, LANE), (N_LANE_TILES, SCAN_SEQS, LANE))
        return jnp.concatenate([lo, hi], axis=1)

    wb = jnp.concatenate([pack_b(bb_re), pack_b(bb_im)], axis=1)
    return (scan_rows(abar_re, abar_re), scan_rows(-abar_im, abar_im), wb, pack_c(c_re), pack_c(-c_im))


def _rope_tables(pos0, length):
    half = RET_DK // 2
    inv = ROPE_BASE ** (-jnp.arange(half, dtype=F32) / half)
    ang = (pos0 + jnp.arange(length)).astype(F32)[:, None] * inv[None, :]
    cos = jnp.cos(ang)
    sin = jnp.sin(ang)
    return jnp.concatenate([cos, cos], axis=1), jnp.concatenate([-sin, sin], axis=1)


def _pack_s5_state(re, im):
    def tiles(a):
        return a.reshape(-1, SCAN_SEQS, N_LANE_TILES, LANE).transpose(0, 2, 1, 3)
    return jnp.concatenate([tiles(re), tiles(im)], axis=2)


def _unpack_s5_state(tiles):
    def seqs(a):
        return a.transpose(0, 2, 1, 3).reshape(-1, S5_GROUPS, S5_STATE)
    return seqs(tiles[:, :, :SCAN_SEQS]), seqs(tiles[:, :, SCAN_SEQS:])


def _mixer(x, tables, s5x0, ret0, w, layer, *, tb):
    nb, length, d = x.shape
    chunk = CHUNK if length >= CHUNK else length
    assert length % tb == 0 and tb % chunk == 0 and tb % 2 == 0 and nb % SCAN_SEQS == 0
    m = SCAN_SEQS * tb
    cos2, sin2 = tables
    consts = [w[k] for k in ('mix_norm_pre', 'mix_norm_post', 'w_in', 'a_rows', 'ais_rows', 'wb', 'wc_re', 'wc_im',
                             's5_d', 's5_w_glu', 's5_b_glu', 's5_out_norm', 'ret_out_norm', 'w_out')]
    x_spec = pl.BlockSpec((SCAN_SEQS, tb, d), lambda g, j: (g, j, 0))
    s5_in = pl.BlockSpec((None, 1) + s5x0.shape[2:], lambda g, j: (layer, g, 0, 0, 0))
    ret_in = pl.BlockSpec((None, SCAN_SEQS) + ret0.shape[2:], lambda g, j: (layer, g, 0, 0, 0))
    s5_out = pl.BlockSpec((1,) + s5x0.shape[2:], lambda g, j: (g, 0, 0, 0))
    ret_out = pl.BlockSpec((SCAN_SEQS,) + ret0.shape[2:], lambda g, j: (g, 0, 0, 0))
    in_specs = ([x_spec,
                 pl.BlockSpec((tb, RET_DK), lambda g, j: (j, 0)),
                 pl.BlockSpec((tb, RET_DK), lambda g, j: (j, 0)),
                 s5_in, ret_in]
                + [_layer_spec(c, layer) for c in consts])
    return pl.pallas_call(
        functools.partial(_mixer_kernel, chunk=chunk),
        grid=(nb // SCAN_SEQS, length // tb),
        in_specs=in_specs,
        out_specs=[x_spec, s5_out, ret_out],
        out_shape=[jax.ShapeDtypeStruct(x.shape, F32), jax.ShapeDtypeStruct(s5x0.shape[1:], F32),
                   jax.ShapeDtypeStruct(ret0.shape[1:], F32)],
        scratch_shapes=[pltpu.VMEM((D_S5 // LANE, m, LANE), F32),
                        pltpu.VMEM((N_LANE_TILES, tb // 2, 2, 2 * SCAN_SEQS, LANE), F32),
                        pltpu.VMEM((N_LANE_TILES, m, LANE), F32),
                        pltpu.VMEM((N_LANE_TILES, m, LANE), F32),
                        pltpu.VMEM((D_S5 // LANE, m, LANE), F32),
                        pltpu.VMEM((m, D_RET), F32)],
        compiler_params=pltpu.CompilerParams(dimension_semantics=("arbitrary", "arbitrary"),
                                             vmem_limit_bytes=VMEM_LIMIT),
        name="mixer",
    )(x, cos2, sin2, s5x0, ret0, *consts)


def _cross_kernel(x_ref, k_ref, v_ref, gpre_ref, gpost_ref, wq_ref, wo_ref, y_ref, o_ref):
    _, tq, d = x_ref.shape
    dh = d // X_HEADS
    subs = _sub_blocks(tq)
    xs = [x_ref[0, rows, :] for rows in subs]
    qs = [_dot(_rms(x, gpre_ref[...]).astype(BF16), wq_ref[...]) * (dh ** -0.5) for x in xs]
    for hd in range(X_HEADS):
        cols = slice(hd * dh, (hd + 1) * dh)
        for rows, q in zip(subs, qs):
            s = _dot_nt(q[:, cols].astype(BF16), k_ref[0, :, cols])
            p = jnp.exp(s - jnp.max(s, axis=-1, keepdims=True))
            l = jnp.sum(p, axis=-1, keepdims=True)
            o_ref[rows, cols] = _dot(p.astype(BF16), v_ref[0, :, cols]) / l
    outs = [_dot(o_ref[rows, :].astype(BF16), wo_ref[...]) for rows in subs]
    for rows, x, out in zip(subs, xs, outs):
        y_ref[0, rows, :] = x + _rms(out, gpost_ref[...])


def _cross(x, mem_k, mem_v, w, layer, *, tq):
    nb, length, d = x.shape
    assert length % tq == 0
    kv_spec = pl.BlockSpec((None, 1) + mem_k.shape[2:], lambda b, j: (layer, b, 0, 0))
    consts = [w[k] for k in ('xattn_norm_pre', 'xattn_norm_post', 'w_cq', 'w_co')]
    return pl.pallas_call(
        _cross_kernel,
        grid=(nb, length // tq),
        in_specs=[pl.BlockSpec((1, tq, d), lambda b, j: (b, j, 0)), kv_spec, kv_spec]
                 + [_layer_spec(c, layer) for c in consts],
        out_specs=pl.BlockSpec((1, tq, d), lambda b, j: (b, j, 0)),
        out_shape=jax.ShapeDtypeStruct(x.shape, F32),
        scratch_shapes=[pltpu.VMEM((tq, d), F32)],
        compiler_params=pltpu.CompilerParams(dimension_semantics=("arbitrary", "arbitrary"),
                                             vmem_limit_bytes=VMEM_LIMIT),
        name="cross",
    )(x, mem_k, mem_v, *consts)


def _cross_cached_kernel(x_ref, k_ref, v_ref, gpre_ref, gpost_ref, wq_ref, wo_ref, y_ref, o_ref):
    ns, tq, d = x_ref.shape
    dh = d // X_HEADS
    x = x_ref[...].reshape(ns * tq, d)
    h = _rms(x, gpre_ref[...]).astype(BF16)
    q = _dot(h, wq_ref[...]) * (dh ** -0.5)
    for hd in range(X_HEADS):
        cols = slice(hd * dh, (hd + 1) * dh)
        qh = q[:, cols].reshape(ns, tq, dh).astype(BF16)
        kh = k_ref[:, :, hd, :].astype(BF16)
        vh = v_ref[:, :, hd, :].astype(BF16)
        st = _bdot(kh, qh, 2, 2)
        p = jnp.exp(st - jnp.max(st, axis=1, keepdims=True))
        p = p / jnp.sum(p, axis=1, keepdims=True)
        o_ref[:, cols] = _bdot(p.astype(BF16), vh, 1, 1).reshape(ns * tq, dh)
    out = _dot(o_ref[...].astype(BF16), wo_ref[...])
    y_ref[...] = (x + _rms(out, gpost_ref[...])).reshape(ns, tq, d)


def _cross_cached(x, cache_k, cache_v, w, layer):
    nb, length, d = x.shape
    ns = SCAN_SEQS if nb % SCAN_SEQS == 0 else 1
    kv_spec = pl.BlockSpec((None, ns) + cache_k.shape[2:], lambda g: (layer, g, 0, 0, 0))
    x_spec = pl.BlockSpec((ns, length, d), lambda g: (g, 0, 0))
    consts = [w[k] for k in ('xattn_norm_pre', 'xattn_norm_post', 'w_cq', 'w_co')]
    return pl.pallas_call(
        _cross_cached_kernel,
        grid=(nb // ns,),
        in_specs=[x_spec, kv_spec, kv_spec] + [_layer_spec(c, layer) for c in consts],
        out_specs=x_spec,
        out_shape=jax.ShapeDtypeStruct(x.shape, F32),
        scratch_shapes=[pltpu.VMEM((ns * length, d), F32)],
        compiler_params=pltpu.CompilerParams(dimension_semantics=("arbitrary",),
                                             vmem_limit_bytes=VMEM_LIMIT),
        name="cross_cached",
    )(x, cache_k, cache_v, *consts)


def _ffn_kernel(x_ref, gpre_ref, gpost_ref, wg_ref, wu_ref, wd_ref, y_ref):
    subs = _sub_blocks(x_ref.shape[0])
    xs = [x_ref[rows, :] for rows in subs]
    hs = [_rms(x, gpre_ref[...]).astype(BF16) for x in xs]
    gs = [_dot(h, wg_ref[...]) for h in hs]
    acts = [(g * _sigmoid(g) * _dot(h, wu_ref[...])).astype(BF16) for g, h in zip(gs, hs)]
    fs = [_dot(a, wd_ref[...]) for a in acts]
    for rows, x, f in zip(subs, xs, fs):
        y_ref[rows, :] = x + _rms(f, gpost_ref[...])


def _ffn(x, w, layer, *, tm):
    shape = x.shape
    d = shape[-1]
    x2 = x.reshape(-1, d)
    rows = x2.shape[0]
    assert rows % tm == 0
    consts = [w[k] for k in ('ffn_norm_pre', 'ffn_norm_post', 'w_gate', 'w_up', 'w_down')]
    y = pl.pallas_call(
        _ffn_kernel,
        grid=(rows // tm,),
        in_specs=[pl.BlockSpec((tm, d), lambda j: (j, 0))] + [_layer_spec(c, layer) for c in consts],
        out_specs=pl.BlockSpec((tm, d), lambda j: (j, 0)),
        out_shape=jax.ShapeDtypeStruct(x2.shape, F32),
        compiler_params=pltpu.CompilerParams(dimension_semantics=("arbitrary",),
                                             vmem_limit_bytes=VMEM_LIMIT),
        name="ffn",
    )(x2, *consts)
    return y.reshape(shape)


def _memkv_kernel(mem_ref, g_ref, wk_ref, wv_ref, k_ref, v_ref, kb_ref, vb_ref):
    dh = k_ref.shape[-1]
    m = _rms(mem_ref[0], g_ref[0]).astype(BF16)
    for w_ref, out_ref, outb_ref in ((wk_ref, k_ref, kb_ref), (wv_ref, v_ref, vb_ref)):
        r = _dot(m, w_ref[0])
        outb_ref[0, 0] = r.astype(BF16)
        for hd in range(X_HEADS):
            out_ref[0, 0, :, hd, :] = r[:, hd * dh:(hd + 1) * dh]


def _memory_kv(mem, g_mem, w_ck, w_cv):
    depth = g_mem.shape[0]
    nb, n_mem, d = mem.shape
    dh = d // X_HEADS
    out = jax.ShapeDtypeStruct((depth, nb, n_mem, X_HEADS, dh), F32)
    outb = jax.ShapeDtypeStruct((depth, nb, n_mem, d), BF16)
    w_spec = pl.BlockSpec((1, d, d), lambda i, b: (i, 0, 0))
    return pl.pallas_call(
        _memkv_kernel,
        grid=(depth, nb),
        in_specs=[pl.BlockSpec((1, n_mem, d), lambda i, b: (b, 0, 0)),
                  pl.BlockSpec((1, 1, d), lambda i, b: (i, 0, 0)), w_spec, w_spec],
        out_specs=[pl.BlockSpec((1, 1, n_mem, X_HEADS, dh), lambda i, b: (i, b, 0, 0, 0))] * 2
                  + [pl.BlockSpec((1, 1, n_mem, d), lambda i, b: (i, b, 0, 0))] * 2,
        out_shape=[out, out, outb, outb],
        compiler_params=pltpu.CompilerParams(dimension_semantics=("arbitrary", "arbitrary"),
                                             vmem_limit_bytes=VMEM_LIMIT),
        name="memory_kv",
    )(mem, g_mem, w_ck, w_cv)


def _prepare_weights(p):
    depth = p['w_in'].shape[0]
    w = {}
    for k in ('w_in', 's5_w_glu', 'w_out', 'w_cq', 'w_co', 'w_gate', 'w_up', 'w_down'):
        w[k] = p[k].astype(BF16)
    for k in ('mix_norm_pre', 'mix_norm_post', 's5_d', 's5_b_glu', 's5_out_norm', 'ret_out_norm',
              'xattn_norm_pre', 'xattn_norm_post', 'ffn_norm_pre', 'ffn_norm_post'):
        w[k] = p[k].reshape(depth, 1, -1).astype(F32)
    (w['a_rows'], w['ais_rows'], w['wb'], w['wc_re'], w['wc_im']) = jax.vmap(_s5_discretize)(
        p['s5_lambda_re'], p['s5_lambda_im'], p['s5_log_step'],
        p['s5_b_re'], p['s5_b_im'], p['s5_c_re'], p['s5_c_im'])
    return w


def _block_rows(length, target):
    return target if length >= target and length % target == 0 else length


def _trunk(x, pos0, mem_k, mem_v, s5_re0, s5_im0, ret0, w):
    nb, length, d = x.shape
    depth = ret0.shape[0]
    tables = _rope_tables(pos0, length)
    tb = _block_rows(length, 128)
    tq = _block_rows(length, 512)
    tm = _block_rows(nb * length, 512)
    s5x0 = jax.vmap(_pack_s5_state)(s5_re0, s5_im0)
    s5r_all, s5i_all, ret_all = [], [], []
    for i in range(depth):
        x, s5x, s_ret = _mixer(x, tables, s5x0, ret0, w, i, tb=tb)
        s5r, s5i = _unpack_s5_state(s5x)
        if mem_k.ndim == 5:
            x = _cross_cached(x, mem_k, mem_v, w, i)
        else:
            x = _cross(x, mem_k, mem_v, w, i, tq=tq)
        x = _ffn(x, w, i, tm=tm)
        s5r_all.append(s5r)
        s5i_all.append(s5i)
        ret_all.append(s_ret)
    return x, jnp.stack(s5r_all), jnp.stack(s5i_all), jnp.stack(ret_all)


def kernel(x_prompt, x_sample, mem_prompt, state_s5_re, state_s5_im, state_ret, cache_mem_k, cache_mem_v,
           mix_norm_pre, mix_norm_post, w_in, s5_lambda_re, s5_lambda_im, s5_log_step,
           s5_b_re, s5_b_im, s5_c_re, s5_c_im, s5_d, s5_w_glu, s5_b_glu, s5_out_norm, ret_out_norm,
           w_out, xattn_norm_pre, xattn_norm_post, mem_norm, w_cq, w_ck, w_cv, w_co,
           ffn_norm_pre, ffn_norm_post, w_gate, w_up, w_down):
    p = {
        'mix_norm_pre': mix_norm_pre, 'mix_norm_post': mix_norm_post, 'w_in': w_in,
        's5_lambda_re': s5_lambda_re, 's5_lambda_im': s5_lambda_im, 's5_log_step': s5_log_step,
        's5_b_re': s5_b_re, 's5_b_im': s5_b_im, 's5_c_re': s5_c_re, 's5_c_im': s5_c_im, 's5_d': s5_d,
        's5_w_glu': s5_w_glu, 's5_b_glu': s5_b_glu, 's5_out_norm': s5_out_norm,
        'ret_out_norm': ret_out_norm, 'w_out': w_out,
        'xattn_norm_pre': xattn_norm_pre, 'xattn_norm_post': xattn_norm_post,
        'w_cq': w_cq, 'w_co': w_co,
        'ffn_norm_pre': ffn_norm_pre, 'ffn_norm_post': ffn_norm_post,
        'w_gate': w_gate, 'w_up': w_up, 'w_down': w_down,
    }
    depth = w_in.shape[0]
    nbp = x_prompt.shape[0]
    d = x_prompt.shape[-1]
    w = _prepare_weights(p)

    mk, mv, mk_bf16, mv_bf16 = _memory_kv(mem_prompt, mem_norm.reshape(depth, 1, d),
                                          w_ck.astype(BF16), w_cv.astype(BF16))
    zeros_s5 = jnp.zeros((depth, nbp, S5_GROUPS, S5_STATE), F32)
    zeros_ret = jnp.zeros((depth, nbp, RET_HEADS, RET_DK, RET_DV), F32)
    y_prompt, s5r_p, s5i_p, ret_p = _trunk(x_prompt, 0, mk_bf16, mv_bf16, zeros_s5, zeros_s5, zeros_ret, w)
    y_sample, s5r_s, s5i_s, ret_s = _trunk(x_sample, PAST_LEN, cache_mem_k, cache_mem_v,
                                           state_s5_re, state_s5_im, state_ret, w)
    return (y_prompt, y_sample, s5r_p, s5i_p, ret_p, mk, mv, s5r_s, s5i_s, ret_s)
```

```python
import functools
import math

import jax
import jax.numpy as jnp
from jax import lax
from jax.experimental import pallas as pl
from jax.experimental.pallas import tpu as pltpu

F32 = jnp.float32
BF16 = jnp.bfloat16

PAST_LEN = 1024
CHUNK = 64
D_S5 = 512
S5_GROUP = 16
S5_GROUPS = D_S5 // S5_GROUP
S5_STATE = 64
S5_LANES = S5_GROUPS * S5_STATE
RET_HEADS = 4
RET_DK = 128
RET_DV = 128
D_RET = RET_HEADS * RET_DV
X_HEADS = 4
ROPE_BASE = 10000.0
EPS = 1e-6

LANE = 128
MXU = 256
N_LANE_TILES = S5_LANES // LANE
SCAN_TILES = 8
SCAN_PIECE = 8
SCAN_SEQS = 4
SUB_BLOCKS = 2
VMEM_LIMIT = 56 * 1024 * 1024


def _layer_spec(arr, layer):
    tail = (0,) * (arr.ndim - 1)
    return pl.BlockSpec((None,) + arr.shape[1:], lambda *_: (layer,) + tail, pipeline_mode=pl.Buffered(1))


def _sub_blocks(rows):
    n = SUB_BLOCKS if rows % (8 * SUB_BLOCKS) == 0 else 1
    return [slice(i * rows // n, (i + 1) * rows // n) for i in range(n)]


def _rms(x, g):
    ms = jnp.mean(x * x, axis=-1, keepdims=True)
    return x * lax.rsqrt(ms + EPS) * g


def _dot(a, b):
    return jnp.dot(a, b, preferred_element_type=F32)


def _dot_nt(a, b):
    return lax.dot_general(a, b, (((1,), (1,)), ((), ())), preferred_element_type=F32)


def _bdot(a, b, ca, cb):
    return lax.dot_general(a, b, (((ca,), (cb,)), ((0,), (0,))), preferred_element_type=F32)


def _gelu_tanh(x):
    c = math.sqrt(2.0 / math.pi)
    return 0.5 * x * (1.0 + jnp.tanh(c * (x + 0.044715 * (x * x * x))))


def _sigmoid(x):
    return 1.0 / (1.0 + jnp.exp(-x))


def _mixer_kernel(x_ref, cos_ref, sin_ref, s5x0_ref, ret0_ref,
                  gpre_ref, gpost_ref, win_ref, ar_ref, ais_ref, wb_ref,
                  wcre_ref, wcim_ref, dskip_ref, wglu_ref, bglu_ref, gs5_ref, gret_ref, wout_ref,
                  y_ref, s5x_ref, ret_ref,
                  ut_ref, bu_ref, xre_ref, xim_ref, yt_ref, reto_ref, *, chunk):
    nb, tb, d = x_ref.shape
    assert nb == SCAN_SEQS
    m = nb * tb
    pairs = tb // 2

    @pl.when(pl.program_id(1) == 0)
    def _():
        s5x_ref[...] = s5x0_ref[...]
        ret_ref[...] = ret0_ref[...]

    x = x_ref[...].reshape(m, d)
    h = _rms(x, gpre_ref[...]).astype(BF16)

    u = _dot(h, win_ref[:, 0:D_S5])
    for lt in range(D_S5 // LANE):
        for b in range(nb):
            ut_ref.at[lt][pl.ds(b, tb, stride=nb), :] = u[b * tb:(b + 1) * tb, lt * LANE:(lt + 1) * LANE]

    lo = lax.broadcasted_iota(jnp.int32, (1, 2 * nb, LANE), 1) < nb
    for ks in range(D_S5 // LANE):
        t3 = ut_ref[ks].reshape(pairs, 2 * nb, LANE)
        r3 = pltpu.roll(t3, nb, axis=1)
        even = jnp.concatenate([jnp.where(lo, t3, 0.0), jnp.where(lo, 0.0, r3)], axis=-1)
        odd = jnp.concatenate([jnp.where(lo, r3, 0.0), jnp.where(lo, 0.0, t3)], axis=-1)
        for parity, lhs3 in ((0, even), (1, odd)):
            lhs = lhs3.reshape(pairs * 2 * nb, 2 * LANE).astype(BF16)
            for n in range(2 * ks, 2 * ks + 2):
                r = _dot(lhs, wb_ref[n]).reshape(pairs, 2 * nb, MXU)
                bu_ref[2 * n, :, parity] = r[:, :, :LANE]
                bu_ref[2 * n + 1, :, parity] = r[:, :, LANE:]

    lo2 = lax.broadcasted_iota(jnp.int32, (2 * nb, LANE), 0) < nb

    def scan_pieces(pairs_per_piece):
        for c0 in range(0, N_LANE_TILES, SCAN_TILES):
            tiles = range(c0, c0 + SCAN_TILES)
            state = {}
            for i in tiles:
                x0 = s5x_ref[0, i]
                state[i] = (x0, pltpu.roll(x0, nb, axis=0))
            for k in range(pairs):
                for i in tiles:
                    xc, xs = state[i]
                    ar = ar_ref[i]
                    ais = ais_ref[i]
                    xe = ar * xc + (ais * xs + bu_ref[i, k, 0])
                    xes = pltpu.roll(xe, nb, axis=0)
                    xo = ar * xe + (ais * xes + bu_ref[i, k, 1])
                    xos = pltpu.roll(xo, nb, axis=0)
                    rows = slice(k * 2 * nb, (k + 1) * 2 * nb)
                    xre_ref[i, rows, :] = jnp.where(lo2, xe, xos)
                    xim_ref[i, rows, :] = jnp.where(lo2, xes, xo)
                    state[i] = (xo, xos)
                if (k + 1) % pairs_per_piece == 0:
                    yield
            for i in tiles:
                s5x_ref[0, i] = state[i][0]

    o1 = D_S5
    o2 = o1 + RET_HEADS * RET_DK
    o3 = o2 + RET_HEADS * RET_DK
    o4 = o3 + D_RET
    scan = scan_pieces(SCAN_PIECE)
    q_all = _dot(h, win_ref[:, o1:o2])
    next(scan, None)
    k_all = _dot(h, win_ref[:, o2:o3])
    next(scan, None)
    v_all = _dot(h, win_ref[:, o3:o4]).astype(BF16)
    next(scan, None)
    g_all = _dot(h, win_ref[:, o4:o4 + D_RET])
    next(scan, None)
    cos2 = cos_ref[...]
    sin2 = sin_ref[...]
    ri = lax.broadcasted_iota(jnp.int32, (tb, tb), 0)
    ci = lax.broadcasted_iota(jnp.int32, (tb, tb), 1)
    same = (ri // chunk) == (ci // chunk)
    earlier = (ci // chunk) < (ri // chunk)
    dist = (ri - ci).astype(F32)
    pos = lax.broadcasted_iota(jnp.int32, (tb, 1), 0).astype(F32)
    gret = gret_ref[...]
    for hd in range(RET_HEADS):
        lg = math.log(1.0 - 2.0 ** (-5.0 - hd))
        dmat = jnp.where(same, jnp.exp(lg * jnp.abs(dist)), jnp.where(earlier, jnp.exp(lg * dist), 0.0))
        qdec = jnp.exp(lg * (pos + 1.0)) * (RET_DK ** -0.5)
        kdec = jnp.exp(lg * (tb - 1.0 - pos))
        gblk = math.exp(lg * tb)
        cols = slice(hd * RET_DK, (hd + 1) * RET_DK)
        qh = q_all[:, cols].reshape(nb, tb, RET_DK)
        kh = k_all[:, cols].reshape(nb, tb, RET_DK)
        vh = v_all[:, cols].reshape(nb, tb, RET_DV)
        qh = qh * cos2 + pltpu.roll(qh, RET_DK // 2, axis=2) * sin2
        kh = kh * cos2 + pltpu.roll(kh, RET_DK // 2, axis=2) * sin2
        s_prev = ret_ref[:, hd]
        sc = _bdot((qh * (RET_DK ** -0.5)).astype(BF16), kh.astype(BF16), 2, 2) * dmat
        o = _bdot(sc.astype(BF16), vh, 2, 1) + _bdot((qh * qdec).astype(BF16), s_prev.astype(BF16), 2, 1)
        ret_ref[:, hd] = gblk * s_prev + _bdot((kh * kdec).astype(BF16), vh, 1, 1)
        o = _rms(o, gret[:, cols])
        gh = g_all[:, cols].reshape(nb, tb, RET_DV)
        reto_ref[:, cols] = (gh * _sigmoid(gh) * o).reshape(m, RET_DV)
        next(scan, None)
        next(scan, None)
        next(scan, None)
    for _ in scan:
        pass

    tiles_per_out = (MXU // S5_GROUP) * S5_STATE // LANE
    ut = jnp.concatenate([ut_ref[lt] for lt in range(D_S5 // LANE)], axis=1)
    for n in range(D_S5 // MXU):
        xr = jnp.concatenate([xre_ref[n * tiles_per_out + i] for i in range(tiles_per_out)], axis=1)
        xi = jnp.concatenate([xim_ref[n * tiles_per_out + i] for i in range(tiles_per_out)], axis=1)
        yn = _dot(xr.astype(BF16), wcre_ref[n]) + _dot(xi.astype(BF16), wcim_ref[n])
        yn = yn + dskip_ref[:, n * MXU:(n + 1) * MXU] * ut[:, n * MXU:(n + 1) * MXU]
        yt_ref[2 * n] = yn[:, :LANE]
        yt_ref[2 * n + 1] = yn[:, LANE:]
    y = jnp.concatenate(
        [jnp.concatenate([yt_ref.at[lt][pl.ds(b, tb, stride=nb), :] for lt in range(D_S5 // LANE)], axis=1)
         for b in range(nb)], axis=0)
    z = _gelu_tanh(y)
    z = z * _sigmoid(_dot(z.astype(BF16), wglu_ref[...]) + bglu_ref[...])
    s5_out = _rms(z, gs5_ref[...])

    mix = _dot(s5_out.astype(BF16), wout_ref[0:D_S5, :]) + _dot(reto_ref[...].astype(BF16), wout_ref[D_S5:, :])
    y_ref[...] = (x + _rms(mix, gpost_ref[...])).reshape(nb, tb, d)


def _s5_discretize(lam_re, lam_im, log_step, b_re, b_im, c_re, c_im):
    dt = jnp.exp(log_step)[:, None]
    ar = lam_re * dt
    ai = lam_im * dt
    mag = jnp.exp(ar)
    abar_re = mag * jnp.cos(ai)
    abar_im = mag * jnp.sin(ai)
    den = lam_re * lam_re + lam_im * lam_im
    nr = abar_re - 1.0
    ni = abar_im
    f_re = (nr * lam_re + ni * lam_im) / den
    f_im = (ni * lam_re - nr * lam_im) / den
    bb_re = f_re[..., None] * b_re - f_im[..., None] * b_im
    bb_im = f_re[..., None] * b_im + f_im[..., None] * b_re

    gpt = MXU // S5_STATE
    gps = LANE // S5_GROUP
    n_tiles = S5_LANES // MXU

    def pack_b(bb):
        bt = bb.transpose(0, 2, 1).reshape(n_tiles, gpt, S5_GROUP, S5_STATE)
        n_idx = jnp.arange(n_tiles)[:, None, None]
        gl = jnp.arange(gps)[None, :, None]
        gi = jnp.arange(gpt)[None, None, :]
        sel = (gl == (n_idx % (gps // gpt)) * gpt + gi).astype(F32)
        w = jnp.einsum('nlg,nghp->nlhgp', sel, bt)
        return w.reshape(n_tiles, LANE, MXU).astype(BF16)

    gpo = MXU // S5_GROUP
    n_out = D_S5 // MXU

    def pack_c(c):
        ct = c.reshape(n_out, gpo, S5_GROUP, S5_STATE)
        eye = jnp.eye(gpo, dtype=F32)
        w = jnp.einsum('lg,nghp->nlpgh', eye, ct)
        return w.reshape(n_out, gpo * S5_STATE, MXU).astype(BF16)

    def scan_rows(lo_half, hi_half):
        lo = jnp.broadcast_to(lo_half.reshape(N_LANE_TILES, 1, LANE), (N_LANE_TILES, SCAN_SEQS, LANE))
        hi = jnp.broadcast_to(hi_half.reshape(N_LANE_TILES, 1, LANE), (N_LANE_TILES, SCAN_SEQS, LANE))
        return jnp.concatenate([lo, hi], axis=1)

    wb = jnp.concatenate([pack_b(bb_re), pack_b(bb_im)], axis=1)
    return (scan_rows(abar_re, abar_re), scan_rows(-abar_im, abar_im), wb, pack_c(c_re), pack_c(-c_im))


def _rope_tables(pos0, length):
    half = RET_DK // 2
    inv = ROPE_BASE ** (-jnp.arange(half, dtype=F32) / half)
    ang = (pos0 + jnp.arange(length)).astype(F32)[:, None] * inv[None, :]
    cos = jnp.cos(ang)
    sin = jnp.sin(ang)
    return jnp.concatenate([cos, cos], axis=1), jnp.concatenate([-sin, sin], axis=1)


def _pack_s5_state(re, im):
    def tiles(a):
        return a.reshape(-1, SCAN_SEQS, N_LANE_TILES, LANE).transpose(0, 2, 1, 3)
    return jnp.concatenate([tiles(re), tiles(im)], axis=2)


def _unpack_s5_state(tiles):
    def seqs(a):
        return a.transpose(0, 2, 1, 3).reshape(-1, S5_GROUPS, S5_STATE)
    return seqs(tiles[:, :, :SCAN_SEQS]), seqs(tiles[:, :, SCAN_SEQS:])


def _mixer(x, tables, s5x0, ret0, w, layer, *, tb):
    nb, length, d = x.shape
    chunk = CHUNK if length >= CHUNK else length
    assert length % tb == 0 and tb % chunk == 0 and tb % 2 == 0 and nb % SCAN_SEQS == 0
    m = SCAN_SEQS * tb
    cos2, sin2 = tables
    consts = [w[k] for k in ('mix_norm_pre', 'mix_norm_post', 'w_in', 'a_rows', 'ais_rows', 'wb', 'wc_re', 'wc_im',
                             's5_d', 's5_w_glu', 's5_b_glu', 's5_out_norm', 'ret_out_norm', 'w_out')]
    x_spec = pl.BlockSpec((SCAN_SEQS, tb, d), lambda g, j: (g, j, 0))
    s5_in = pl.BlockSpec((None, 1) + s5x0.shape[2:], lambda g, j: (layer, g, 0, 0, 0))
    ret_in = pl.BlockSpec((None, SCAN_SEQS) + ret0.shape[2:], lambda g, j: (layer, g, 0, 0, 0))
    s5_out = pl.BlockSpec((1,) + s5x0.shape[2:], lambda g, j: (g, 0, 0, 0))
    ret_out = pl.BlockSpec((SCAN_SEQS,) + ret0.shape[2:], lambda g, j: (g, 0, 0, 0))
    in_specs = ([x_spec,
                 pl.BlockSpec((tb, RET_DK), lambda g, j: (j, 0)),
                 pl.BlockSpec((tb, RET_DK), lambda g, j: (j, 0)),
                 s5_in, ret_in]
                + [_layer_spec(c, layer) for c in consts])
    return pl.pallas_call(
        functools.partial(_mixer_kernel, chunk=chunk),
        grid=(nb // SCAN_SEQS, length // tb),
        in_specs=in_specs,
        out_specs=[x_spec, s5_out, ret_out],
        out_shape=[jax.ShapeDtypeStruct(x.shape, F32), jax.ShapeDtypeStruct(s5x0.shape[1:], F32),
                   jax.ShapeDtypeStruct(ret0.shape[1:], F32)],
        scratch_shapes=[pltpu.VMEM((D_S5 // LANE, m, LANE), F32),
                        pltpu.VMEM((N_LANE_TILES, tb // 2, 2, 2 * SCAN_SEQS, LANE), F32),
                        pltpu.VMEM((N_LANE_TILES, m, LANE), F32),
                        pltpu.VMEM((N_LANE_TILES, m, LANE), F32),
                        pltpu.VMEM((D_S5 // LANE, m, LANE), F32),
                        pltpu.VMEM((m, D_RET), F32)],
        compiler_params=pltpu.CompilerParams(dimension_semantics=("arbitrary", "arbitrary"),
                                             vmem_limit_bytes=VMEM_LIMIT),
        name="mixer",
    )(x, cos2, sin2, s5x0, ret0, *consts)


def _cross_kernel(x_ref, k_ref, v_ref, gpre_ref, gpost_ref, wq_ref, wo_ref, y_ref, o_ref):
    _, tq, d = x_ref.shape
    dh = d // X_HEADS
    subs = _sub_blocks(tq)
    xs = [x_ref[0, rows, :] for rows in subs]
    qs = [_dot(_rms(x, gpre_ref[...]).astype(BF16), wq_ref[...]) * (dh ** -0.5) for x in xs]
    for hd in range(X_HEADS):
        cols = slice(hd * dh, (hd + 1) * dh)
        for rows, q in zip(subs, qs):
            s = _dot_nt(q[:, cols].astype(BF16), k_ref[0, :, cols])
            p = jnp.exp(s - jnp.max(s, axis=-1, keepdims=True))
            l = jnp.sum(p, axis=-1, keepdims=True)
            o_ref[rows, cols] = _dot(p.astype(BF16), v_ref[0, :, cols]) / l
    outs = [_dot(o_ref[rows, :].astype(BF16), wo_ref[...]) for rows in subs]
    for rows, x, out in zip(subs, xs, outs):
        y_ref[0, rows, :] = x + _rms(out, gpost_ref[...])


def _cross(x, mem_k, mem_v, w, layer, *, tq):
    nb, length, d = x.shape
    assert length % tq == 0
    kv_spec = pl.BlockSpec((None, 1) + mem_k.shape[2:], lambda b, j: (layer, b, 0, 0))
    consts = [w[k] for k in ('xattn_norm_pre', 'xattn_norm_post', 'w_cq', 'w_co')]
    return pl.pallas_call(
        _cross_kernel,
        grid=(nb, length // tq),
        in_specs=[pl.BlockSpec((1, tq, d), lambda b, j: (b, j, 0)), kv_spec, kv_spec]
                 + [_layer_spec(c, layer) for c in consts],
        out_specs=pl.BlockSpec((1, tq, d), lambda b, j: (b, j, 0)),
        out_shape=jax.ShapeDtypeStruct(x.shape, F32),
        scratch_shapes=[pltpu.VMEM((tq, d), F32)],
        compiler_params=pltpu.CompilerParams(dimension_semantics=("arbitrary", "arbitrary"),
                                             vmem_limit_bytes=VMEM_LIMIT),
        name="cross",
    )(x, mem_k, mem_v, *consts)


def _cross_cached_kernel(x_ref, k_ref, v_ref, gpre_ref, gpost_ref, wq_ref, wo_ref, y_ref, o_ref):
    ns, tq, d = x_ref.shape
    dh = d // X_HEADS
    x = x_ref[...].reshape(ns * tq, d)
    h = _rms(x, gpre_ref[...]).astype(BF16)
    q = _dot(h, wq_ref[...]) * (dh ** -0.5)
    for hd in range(X_HEADS):
        cols = slice(hd * dh, (hd + 1) * dh)
        qh = q[:, cols].reshape(ns, tq, dh).astype(BF16)
        kh = k_ref[:, :, hd, :].astype(BF16)
        vh = v_ref[:, :, hd, :].astype(BF16)
        st = _bdot(kh, qh, 2, 2)
        p = jnp.exp(st - jnp.max(st, axis=1, keepdims=True))
        p = p / jnp.sum(p, axis=1, keepdims=True)
        o_ref[:, cols] = _bdot(p.astype(BF16), vh, 1, 1).reshape(ns * tq, dh)
    out = _dot(o_ref[...].astype(BF16), wo_ref[...])
    y_ref[...] = (x + _rms(out, gpost_ref[...])).reshape(ns, tq, d)


def _cross_cached(x, cache_k, cache_v, w, layer):
    nb, length, d = x.shape
    ns = SCAN_SEQS if nb % SCAN_SEQS == 0 else 1
    kv_spec = pl.BlockSpec((None, ns) + cache_k.shape[2:], lambda g: (layer, g, 0, 0, 0))
    x_spec = pl.BlockSpec((ns, length, d), lambda g: (g, 0, 0))
    consts = [w[k] for k in ('xattn_norm_pre', 'xattn_norm_post', 'w_cq', 'w_co')]
    return pl.pallas_call(
        _cross_cached_kernel,
        grid=(nb // ns,),
        in_specs=[x_spec, kv_spec, kv_spec] + [_layer_spec(c, layer) for c in consts],
        out_specs=x_spec,
        out_shape=jax.ShapeDtypeStruct(x.shape, F32),
        scratch_shapes=[pltpu.VMEM((ns * length, d), F32)],
        compiler_params=pltpu.CompilerParams(dimension_semantics=("arbitrary",),
                                             vmem_limit_bytes=VMEM_LIMIT),
        name="cross_cached",
    )(x, cache_k, cache_v, *consts)


def _ffn_kernel(x_ref, gpre_ref, gpost_ref, wg_ref, wu_ref, wd_ref, y_ref):
    subs = _sub_blocks(x_ref.shape[0])
    xs = [x_ref[rows, :] for rows in subs]
    hs = [_rms(x, gpre_ref[...]).astype(BF16) for x in xs]
    gs = [_dot(h, wg_ref[...]) for h in hs]
    acts = [(g * _sigmoid(g) * _dot(h, wu_ref[...])).astype(BF16) for g, h in zip(gs, hs)]
    fs = [_dot(a, wd_ref[...]) for a in acts]
    for rows, x, f in zip(subs, xs, fs):
        y_ref[rows, :] = x + _rms(f, gpost_ref[...])


def _ffn(x, w, layer, *, tm):
    shape = x.shape
    d = shape[-1]
    x2 = x.reshape(-1, d)
    rows = x2.shape[0]
    assert rows % tm == 0
    consts = [w[k] for k in ('ffn_norm_pre', 'ffn_norm_post', 'w_gate', 'w_up', 'w_down')]
    y = pl.pallas_call(
        _ffn_kernel,
        grid=(rows // tm,),
        in_specs=[pl.BlockSpec((tm, d), lambda j: (j, 0))] + [_layer_spec(c, layer) for c in consts],
        out_specs=pl.BlockSpec((tm, d), lambda j: (j, 0)),
        out_shape=jax.ShapeDtypeStruct(x2.shape, F32),
        compiler_params=pltpu.CompilerParams(dimension_semantics=("arbitrary",),
                                             vmem_limit_bytes=VMEM_LIMIT),
        name="ffn",
    )(x2, *consts)
    return y.reshape(shape)


def _memkv_kernel(mem_ref, g_ref, wk_ref, wv_ref, k_ref, v_ref, kb_ref, vb_ref):
    dh = k_ref.shape[-1]
    m = _rms(mem_ref[0], g_ref[0]).astype(BF16)
    for w_ref, out_ref, outb_ref in ((wk_ref, k_ref, kb_ref), (wv_ref, v_ref, vb_ref)):
        r = _dot(m, w_ref[0])
        outb_ref[0, 0] = r.astype(BF16)
        for hd in range(X_HEADS):
            out_ref[0, 0, :, hd, :] = r[:, hd * dh:(hd + 1) * dh]


def _memory_kv(mem, g_mem, w_ck, w_cv):
    depth = g_mem.shape[0]
    nb, n_mem, d = mem.shape
    dh = d // X_HEADS
    out = jax.ShapeDtypeStruct((depth, nb, n_mem, X_HEADS, dh), F32)
    outb = jax.ShapeDtypeStruct((depth, nb, n_mem, d), BF16)
    w_spec = pl.BlockSpec((1, d, d), lambda i, b: (i, 0, 0))
    return pl.pallas_call(
        _memkv_kernel,
        grid=(depth, nb),
        in_specs=[pl.BlockSpec((1, n_mem, d), lambda i, b: (b, 0, 0)),
                  pl.BlockSpec((1, 1, d), lambda i, b: (i, 0, 0)), w_spec, w_spec],
        out_specs=[pl.BlockSpec((1, 1, n_mem, X_HEADS, dh), lambda i, b: (i, b, 0, 0, 0))] * 2
                  + [pl.BlockSpec((1, 1, n_mem, d), lambda i, b: (i, b, 0, 0))] * 2,
        out_shape=[out, out, outb, outb],
        compiler_params=pltpu.CompilerParams(dimension_semantics=("arbitrary", "arbitrary"),
                                             vmem_limit_bytes=VMEM_LIMIT),
        name="memory_kv",
    )(mem, g_mem, w_ck, w_cv)


def _prepare_weights(p):
    depth = p['w_in'].shape[0]
    w = {}
    for k in ('w_in', 's5_w_glu', 'w_out', 'w_cq', 'w_co', 'w_gate', 'w_up', 'w_down'):
        w[k] = p[k].astype(BF16)
    for k in ('mix_norm_pre', 'mix_norm_post', 's5_d', 's5_b_glu', 's5_out_norm', 'ret_out_norm',
              'xattn_norm_pre', 'xattn_norm_post', 'ffn_norm_pre', 'ffn_norm_post'):
        w[k] = p[k].reshape(depth, 1, -1).astype(F32)
    (w['a_rows'], w['ais_rows'], w['wb'], w['wc_re'], w['wc_im']) = jax.vmap(_s5_discretize)(
        p['s5_lambda_re'], p['s5_lambda_im'], p['s5_log_step'],
        p['s5_b_re'], p['s5_b_im'], p['s5_c_re'], p['s5_c_im'])
    return w


def _block_rows(length, target):
    return target if length >= target and length % target == 0 else length


def _trunk(x, pos0, mem_k, mem_v, s5_re0, s5_im0, ret0, w):
    nb, length, d = x.shape
    depth = ret0.shape[0]
    tables = _rope_tables(pos0, length)
    tb = _block_rows(length, 128)
    tq = _block_rows(length, 1024)
    tm = _block_rows(nb * length, 512)
    s5x0 = jax.vmap(_pack_s5_state)(s5_re0, s5_im0)
    s5r_all, s5i_all, ret_all = [], [], []
    for i in range(depth):
        x, s5x, s_ret = _mixer(x, tables, s5x0, ret0, w, i, tb=tb)
        s5r, s5i = _unpack_s5_state(s5x)
        if mem_k.ndim == 5:
            x = _cross_cached(x, mem_k, mem_v, w, i)
        else:
            x = _cross(x, mem_k, mem_v, w, i, tq=tq)
        x = _ffn(x, w, i, tm=tm)
        s5r_all.append(s5r)
        s5i_all.append(s5i)
        ret_all.append(s_ret)
    return x, jnp.stack(s5r_all), jnp.stack(s5i_all), jnp.stack(ret_all)


def kernel(x_prompt, x_sample, mem_prompt, state_s5_re, state_s5_im, state_ret, cache_mem_k, cache_mem_v,
           mix_norm_pre, mix_norm_post, w_in, s5_lambda_re, s5_lambda_im, s5_log_step,
           s5_b_re, s5_b_im, s5_c_re, s5_c_im, s5_d, s5_w_glu, s5_b_glu, s5_out_norm, ret_out_norm,
           w_out, xattn_norm_pre, xattn_norm_post, mem_norm, w_cq, w_ck, w_cv, w_co,
           ffn_norm_pre, ffn_norm_post, w_gate, w_up, w_down):
    p = {
        'mix_norm_pre': mix_norm_pre, 'mix_norm_post': mix_norm_post, 'w_in': w_in,
        's5_lambda_re': s5_lambda_re, 's5_lambda_im': s5_lambda_im, 's5_log_step': s5_log_step,
        's5_b_re': s5_b_re, 's5_b_im': s5_b_im, 's5_c_re': s5_c_re, 's5_c_im': s5_c_im, 's5_d': s5_d,
        's5_w_glu': s5_w_glu, 's5_b_glu': s5_b_glu, 's5_out_norm': s5_out_norm,
        'ret_out_norm': ret_out_norm, 'w_out': w_out,
        'xattn_norm_pre': xattn_norm_pre, 'xattn_norm_post': xattn_norm_post,
        'w_cq': w_cq, 'w_co': w_co,
        'ffn_norm_pre': ffn_norm_pre, 'ffn_norm_post': ffn_norm_post,
        'w_gate': w_gate, 'w_up': w_up, 'w_down': w_down,
    }
    depth = w_in.shape[0]
    nbp = x_prompt.shape[0]
    d = x_prompt.shape[-1]
    w = _prepare_weights(p)

    mk, mv, mk_bf16, mv_bf16 = _memory_kv(mem_prompt, mem_norm.reshape(depth, 1, d),
                                          w_ck.astype(BF16), w_cv.astype(BF16))
    zeros_s5 = jnp.zeros((depth, nbp, S5_GROUPS, S5_STATE), F32)
    zeros_ret = jnp.zeros((depth, nbp, RET_HEADS, RET_DK, RET_DV), F32)
    y_prompt, s5r_p, s5i_p, ret_p = _trunk(x_prompt, 0, mk_bf16, mv_bf16, zeros_s5, zeros_s5, zeros_ret, w)
    y_sample, s5r_s, s5i_s, ret_s = _trunk(x_sample, PAST_LEN, cache_mem_k, cache_mem_v,
                                           state_s5_re, state_s5_im, state_ret, w)
    return (y_prompt, y_sample, s5r_p, s5i_p, ret_p, mk, mv, s5r_s, s5i_s, ret_s)
```
